```python
import math
import jax, jax.numpy as jnp
from jax import lax
import numpy as np

D_MODEL = 4096
BATCH = 32
SEQ = 256
DEPTH = 2
DEC_BATCH = 8
DEC_SEQ = 4096
PAST_LEN = 256

GRID_W = 64
N_BRANCH = 3
BRANCH_W = D_MODEL // 2
W_H = BRANCH_W
HYENA_ORDER = 2
FILTER_BANDS = 16
FILTER_FEAT = 1 + 2 * FILTER_BANDS
FILTER_HIDDEN = 64
H_M = 8
DV_M = BRANCH_W // H_M
DK_M = DV_M // 2
H_G = 16
DK_G = BRANCH_W // H_G
DV_G = BRANCH_W // H_G
D_FF = 11008
CHUNK = 64
EPS = 1e-6
IN_SPLITS = (3 * W_H,
             H_M * DK_M, H_M * DK_M,
             H_M * DV_M, H_M * DV_M,
             4 * H_M,
             H_G * (2 * DK_G + DV_G),
             H_G * DV_G,
             2 * H_G, 2 * H_G,
             N_BRANCH * D_MODEL)
N_IN = sum(IN_SPLITS)
F32 = jnp.float32

kernel_name = 'hybrid_hyena_mlstm_gdn_diffusion_step'


def rmsnorm(x, g):
    xf = x.astype(F32)
    y = xf * lax.rsqrt(jnp.mean(xf * xf, axis=-1, keepdims=True) + EPS)
    return (y * g.astype(F32)).astype(x.dtype)


def head_rmsnorm(h, g):
    return h * lax.rsqrt(jnp.mean(h * h, axis=-1, keepdims=True) + EPS) * g.astype(F32)


def l2norm(x):
    return x * lax.rsqrt(jnp.sum(x * x, axis=-1, keepdims=True) + EPS)


def ada_mod(cvec, w, b):
    m = jax.nn.silu(cvec) @ w + b
    return [t[..., None, :] for t in jnp.split(m, 6, axis=-1)]


def conv1d_centred(x, w):
    xp = jnp.pad(x, ((0, 0), (1, 1), (0, 0)))
    return xp[:, :-2] * w[0] + xp[:, 1:-1] * w[1] + xp[:, 2:] * w[2]


def dwconv_grid(x, w, b, rows):
    B, L, C = x.shape
    y = lax.conv_general_dilated(x.reshape(B, rows, L // rows, C), w[:, :, None, :].astype(x.dtype),
                                 window_strides=(1, 1), padding='SAME',
                                 dimension_numbers=('NHWC', 'HWIO', 'NHWC'), feature_group_count=C)
    return y.reshape(B, L, C) + b.astype(x.dtype)


def to_chunks(a):
    B, L = a.shape[:2]
    return jnp.moveaxis(a.reshape((B, L // CHUNK, CHUNK) + a.shape[2:]), 1, 0)


def from_chunks(a):
    nc, B, T = a.shape[:3]
    return jnp.moveaxis(a, 0, 1).reshape((B, nc * T) + a.shape[3:])


def flip(a):
    return jnp.flip(a, axis=1)


def hyena_filters(L, w1, b1, w2, b2, freq, w3):
    t = jnp.arange(L, dtype=F32) / L
    bands = jnp.linspace(1e-4, FILTER_BANDS - 1, FILTER_BANDS, dtype=F32)
    ang = 2.0 * math.pi * t[:, None] * bands[None, :]
    feat = jnp.concatenate([t[:, None], jnp.cos(ang), jnp.sin(ang)], axis=-1)
    fr = freq.astype(F32)
    hid = jnp.sin(fr * (feat @ w1.astype(F32) + b1.astype(F32)))
    hid = jnp.sin(fr * (hid @ w2.astype(F32) + b2.astype(F32)))
    h = (hid @ w3.astype(F32)).reshape(L, HYENA_ORDER, 2, W_H)
    deltas = jnp.abs(jnp.linspace(math.log(1e-2) / 1.5, math.log(1e-2) / 0.3, W_H, dtype=F32))
    h = h * jnp.exp(-t[:, None] * deltas[None, :])[:, None, None, :]
    return h * lax.rsqrt(jnp.sum(h * h, axis=(0, 2), keepdims=True) + EPS)


def fft_longconv(u, k_fwd, k_bwd, bias):
    L, C = k_fwd.shape
    k = jnp.concatenate([k_fwd[:1] + k_bwd[:1], k_fwd[1:], jnp.zeros((1, C), F32), k_bwd[:0:-1]], axis=0)
    y = jnp.fft.irfft(jnp.fft.rfft(u, n=2 * L, axis=1) * jnp.fft.rfft(k, axis=0)[None], n=2 * L, axis=1)[:, :L]
    return y + u * bias.astype(F32)


def hyena_mixer(zh, conv_w, filt, bias):
    zc = conv1d_centred(zh, conv_w.astype(zh.dtype)).astype(F32)
    v, x1, x2 = jnp.split(zc, 3, axis=-1)
    s = x1 * fft_longconv(v, filt[:, 0, 0], filt[:, 0, 1], bias[0])
    return x2 * fft_longconv(s, filt[:, 1, 0], filt[:, 1, 1], bias[1])


def mlstm_scan(q, k, v, log_i, log_f, C0, n0, m0):
    causal = jnp.tril(jnp.ones((CHUNK, CHUNK), dtype=bool))[None, :, :, None]

    def step(carry, inp):
        C, n, m = carry
        qc, kc, vc, ic, fc = inp
        b = jnp.cumsum(fc, axis=1)
        dmat = jnp.where(causal, b[:, :, None, :] - b[:, None, :, :] + ic[:, None, :, :], -jnp.inf)
        inter = b + m[:, None, :]
        m_t = jnp.maximum(inter, jnp.max(dmat, axis=2))
        s = jnp.einsum('bthd,bshd->btsh', qc, kc) * jnp.exp(dmat - m_t[:, :, None, :])
        w_inter = jnp.exp(inter - m_t)
        num = jnp.einsum('btsh,bshv->bthv', s, vc) + w_inter[..., None] * jnp.einsum('bthd,bhdv->bthv', qc, C)
        den = jnp.sum(s, axis=2) + w_inter * jnp.einsum('bthd,bhd->bth', qc, n)
        h = num / jnp.maximum(jnp.abs(den), jnp.exp(-m_t))[..., None]
        b_end = b[:, -1]
        g = b_end[:, None, :] - b + ic
        m_new = jnp.maximum(b_end + m, jnp.max(g, axis=1))
        wg = jnp.exp(g - m_new[:, None, :])
        carry_decay = jnp.exp(b_end + m - m_new)
        C = carry_decay[..., None, None] * C + jnp.einsum('bsh,bshd,bshv->bhdv', wg, kc, vc)
        n = carry_decay[..., None] * n + jnp.einsum('bsh,bshd->bhd', wg, kc)
        return (C, n, m_new), h

    xs = tuple(to_chunks(a) for a in (q, k, v, log_i, log_f))
    state, h = lax.scan(step, (C0, n0, m0), xs)
    return from_chunks(h), state


def mlstm_mixer(mq, mk, mv, mo, mg, gate_b, norm_g, C0, n0, m0):
    B, L, _ = mq.shape
    q = mq.astype(F32).reshape(B, L, H_M, DK_M)
    k = mk.astype(F32).reshape(B, L, H_M, DK_M) * DK_M ** -0.5
    v = mv.astype(F32).reshape(B, L, H_M, DV_M)
    gp = mg.astype(F32).reshape(B, L, 4, H_M) + gate_b.astype(F32)
    log_i = gp[:, :, 0:2]
    log_f = jax.nn.log_sigmoid(gp[:, :, 2:4])
    hf, (Cf, nf, mf) = mlstm_scan(q, k, v, log_i[:, :, 0], log_f[:, :, 0], C0[:, 0], n0[:, 0], m0[:, 0])
    hb, (Cb, nb, mb) = mlstm_scan(flip(q), flip(k), flip(v), flip(log_i[:, :, 1]), flip(log_f[:, :, 1]),
                                  C0[:, 1], n0[:, 1], m0[:, 1])
    h = head_rmsnorm(hf + flip(hb), norm_g)
    y = h.reshape(B, L, H_M * DV_M) * jax.nn.sigmoid(mo.astype(F32))
    return y, (jnp.stack([Cf, Cb], axis=1), jnp.stack([nf, nb], axis=1), jnp.stack([mf, mb], axis=1))


def gdn_scan(q, k, v, g, beta, S0):
    tri = jnp.tril(jnp.ones((CHUNK, CHUNK), dtype=bool))
    strict = jnp.tril(jnp.ones((CHUNK, CHUNK), dtype=bool), k=-1)
    eye = jnp.eye(CHUNK, dtype=F32)

    def step(S, inp):
        qc, kc, vc, gc, bc = inp
        G = jnp.swapaxes(jnp.cumsum(gc, axis=1), 1, 2)
        bh = jnp.swapaxes(bc, 1, 2)
        dec = jnp.exp(jnp.where(tri, G[..., :, None] - G[..., None, :], -jnp.inf))
        kk = jnp.einsum('bthd,bshd->bhts', kc, kc)
        a = eye + jnp.where(strict, bh[..., :, None] * kk * dec, 0.0)
        eG = jnp.exp(G)
        rhs = jnp.concatenate([(bh * eG)[..., None] * jnp.swapaxes(kc, 1, 2),
                               bh[..., None] * jnp.swapaxes(vc, 1, 2)], axis=-1)
        wu = lax.linalg.triangular_solve(a, rhs, left_side=True, lower=True, unit_diagonal=True)
        w, u = wu[..., :DK_G], wu[..., DK_G:]
        v_new = u - jnp.einsum('bhtk,bhkv->bhtv', w, S)
        qk = jnp.einsum('bthk,bshk->bhts', qc, kc) * dec
        o = eG[..., None] * jnp.einsum('bthk,bhkv->bhtv', qc, S) + jnp.einsum('bhts,bhsv->bhtv', qk, v_new)
        G_end = G[..., -1]
        S = jnp.exp(G_end)[..., None, None] * S + jnp.einsum('bhs,bshk,bhsv->bhkv',
                                                             jnp.exp(G_end[..., None] - G), kc, v_new)
        return S, jnp.swapaxes(o, 1, 2)

    xs = tuple(to_chunks(a) for a in (q, k, v, g, beta))
    S, o = lax.scan(step, S0, xs)
    return from_chunks(o), S


def gdn_mixer(gqkv, gz, gb, ga, conv_w, A_log, dt_bias, norm_g, S0):
    B, L, _ = gqkv.shape
    qkv = jax.nn.silu(conv1d_centred(gqkv, conv_w.astype(gqkv.dtype)).astype(F32))
    q, k, v = jnp.split(qkv, [H_G * DK_G, 2 * H_G * DK_G], axis=-1)
    q = l2norm(q.reshape(B, L, H_G, DK_G)) * DK_G ** -0.5
    k = l2norm(k.reshape(B, L, H_G, DK_G))
    v = v.reshape(B, L, H_G, DV_G)
    beta = jax.nn.sigmoid(gb.astype(F32).reshape(B, L, 2, H_G))
    g = -jnp.exp(A_log.astype(F32)) * jax.nn.softplus(ga.astype(F32).reshape(B, L, 2, H_G) + dt_bias.astype(F32))
    of, Sf = gdn_scan(q, k, v, g[:, :, 0], beta[:, :, 0], S0[:, 0])
    ob, Sb = gdn_scan(flip(q), flip(k), flip(v), flip(g[:, :, 1]), flip(beta[:, :, 1]), S0[:, 1])
    o = head_rmsnorm(of + flip(ob), norm_g)
    y = o.reshape(B, L, H_G * DV_G) * jax.nn.silu(gz.astype(F32))
    return y, jnp.stack([Sf, Sb], axis=1)


def mixing_block(h, p, states):
    B, L, _ = h.shape
    mC, mn, mm, gS = states
    offsets = np.cumsum(IN_SPLITS)[:-1].tolist()
    zh, mq, mk, mv, mo, mg, gqkv, gz, gb, ga, zm = jnp.split(h @ p['w_in'], offsets, axis=-1)
    filt = hyena_filters(L, p['hyena_w1'], p['hyena_b1'], p['hyena_w2'], p['hyena_b2'], p['hyena_freq'], p['hyena_w3'])
    y_a = hyena_mixer(zh, p['hyena_conv_w'], filt, p['hyena_bias'])
    y_b, m_state = mlstm_mixer(mq, mk, mv, mo, mg, p['mlstm_gate_b'], p['mlstm_norm'], mC, mn, mm)
    y_c, g_state = gdn_mixer(gqkv, gz, gb, ga, p['gdn_conv_w'], p['gdn_A_log'], p['gdn_dt_bias'], p['gdn_norm'], gS)
    gates = jax.nn.sigmoid(zm.astype(F32)).reshape(B, L, N_BRANCH, D_MODEL)
    dt = h.dtype
    wb = p['w_branch']
    mix = (gates[:, :, 0] * (y_a.astype(dt) @ wb[0]).astype(F32)
           + gates[:, :, 1] * (y_b.astype(dt) @ wb[1]).astype(F32)
           + gates[:, :, 2] * (y_c.astype(dt) @ wb[2]).astype(F32))
    return mix.astype(dt) @ p['w_out'], m_state + (g_state,)


def trunk_layer(x, mod, p, rows, states):
    sh1, sc1, g1, sh2, sc2, g2 = mod
    h = rmsnorm(x, p['norm_mix']) * (1.0 + sc1) + sh1
    mix, new_states = mixing_block(h, p, states)
    x = x + g1 * mix
    h = rmsnorm(x, p['norm_ffn']) * (1.0 + sc2) + sh2
    gate = dwconv_grid(h @ p['ffn_w_gate'], p['ffn_conv_w'], p['ffn_conv_b'], rows)
    x = x + g2 * ((jax.nn.silu(gate) * (h @ p['ffn_w_up'])) @ p['ffn_w_down'])
    return x, new_states


def setup_inputs(seed: int = 0) -> dict:
    key = jax.random.key(seed)
    ks = iter(jax.random.split(key, 48))

    def nrm(shape, scale):
        return jax.random.normal(next(ks), shape, F32) * scale

    def gain(shape):
        return 1.0 + nrm(shape, 0.02)

    x_prompt = nrm((BATCH, SEQ, D_MODEL), 1.0)
    x_sample = nrm((DEC_BATCH, DEC_SEQ, D_MODEL), 1.0)
    state_mlstm_C = nrm((DEC_BATCH, DEPTH, 2, H_M, DK_M, DV_M), 0.2)
    state_mlstm_n = nrm((DEC_BATCH, DEPTH, 2, H_M, DK_M), 0.2)
    state_mlstm_m = jax.random.uniform(next(ks), (DEC_BATCH, DEPTH, 2, H_M), F32, 0.0, 2.0)
    state_gdn_S = nrm((DEC_BATCH, DEPTH, 2, H_G, DK_G, DV_G), 0.5)
    c = nrm((DEC_BATCH, D_MODEL), 1.0)
    c_ctx = nrm((D_MODEL,), 1.0)
    ada_w = nrm((DEPTH, D_MODEL, 6 * D_MODEL), 0.5 * D_MODEL ** -0.5)
    ada_b = nrm((DEPTH, 6 * D_MODEL), 0.02)
    norm_mix = gain((DEPTH, D_MODEL))
    norm_ffn = gain((DEPTH, D_MODEL))
    w_in = nrm((DEPTH, D_MODEL, N_IN), D_MODEL ** -0.5)
    hyena_conv_w = nrm((DEPTH, 3, 3 * W_H), 0.5)
    hyena_w1 = nrm((DEPTH, FILTER_FEAT, FILTER_HIDDEN), FILTER_FEAT ** -0.5)
    hyena_b1 = nrm((DEPTH, FILTER_HIDDEN), 0.02)
    hyena_w2 = nrm((DEPTH, FILTER_HIDDEN, FILTER_HIDDEN), FILTER_HIDDEN ** -0.5)
    hyena_b2 = nrm((DEPTH, FILTER_HIDDEN), 0.02)
    hyena_freq = gain((DEPTH, FILTER_HIDDEN))
    hyena_w3 = nrm((DEPTH, FILTER_HIDDEN, HYENA_ORDER * 2 * W_H), FILTER_HIDDEN ** -0.5)
    hyena_bias = nrm((DEPTH, HYENA_ORDER, W_H), 0.5)
    mlstm_gate_b = jnp.concatenate([nrm((DEPTH, 2, H_M), 0.1),
                                    jnp.linspace(3.0, 6.0, H_M, dtype=F32) + nrm((DEPTH, 2, H_M), 0.1)], axis=1)
    mlstm_norm = gain((DEPTH, H_M, DV_M))
    gdn_conv_w = nrm((DEPTH, 3, H_G * (2 * DK_G + DV_G)), 0.5)
    gdn_A_log = jnp.log(jax.random.uniform(next(ks), (DEPTH, 2, H_G), F32, 1.0, 16.0))
    dt_init = jnp.exp(jax.random.uniform(next(ks), (DEPTH, 2, H_G), F32, math.log(1e-3), math.log(1e-1)))
    gdn_dt_bias = dt_init + jnp.log(-jnp.expm1(-dt_init))
    gdn_norm = gain((DEPTH, DV_G))
    w_branch = nrm((DEPTH, N_BRANCH, BRANCH_W, D_MODEL), BRANCH_W ** -0.5)
    w_out = nrm((DEPTH, D_MODEL, D_MODEL), D_MODEL ** -0.5)
    ffn_w_gate = nrm((DEPTH, D_MODEL, D_FF), D_MODEL ** -0.5)
    ffn_w_up = nrm((DEPTH, D_MODEL, D_FF), D_MODEL ** -0.5)
    ffn_conv_w = nrm((DEPTH, 3, 3, D_FF), 1.0 / 3.0)
    ffn_conv_b = nrm((DEPTH, D_FF), 0.02)
    ffn_w_down = nrm((DEPTH, D_FF, D_MODEL), D_FF ** -0.5)
    final_norm = gain((D_MODEL,))
    return {'x_prompt': x_prompt, 'x_sample': x_sample,
            'state_mlstm_C': state_mlstm_C, 'state_mlstm_n': state_mlstm_n,
            'state_mlstm_m': state_mlstm_m, 'state_gdn_S': state_gdn_S,
            'c': c, 'c_ctx': c_ctx, 'ada_w': ada_w, 'ada_b': ada_b,
            'norm_mix': norm_mix, 'norm_ffn': norm_ffn, 'w_in': w_in,
            'hyena_conv_w': hyena_conv_w, 'hyena_w1': hyena_w1, 'hyena_b1': hyena_b1,
            'hyena_w2': hyena_w2, 'hyena_b2': hyena_b2, 'hyena_freq': hyena_freq,
            'hyena_w3': hyena_w3, 'hyena_bias': hyena_bias,
            'mlstm_gate_b': mlstm_gate_b, 'mlstm_norm': mlstm_norm,
            'gdn_conv_w': gdn_conv_w, 'gdn_A_log': gdn_A_log, 'gdn_dt_bias': gdn_dt_bias, 'gdn_norm': gdn_norm,
            'w_branch': w_branch, 'w_out': w_out,
            'ffn_w_gate': ffn_w_gate, 'ffn_w_up': ffn_w_up, 'ffn_conv_w': ffn_conv_w,
            'ffn_conv_b': ffn_conv_b, 'ffn_w_down': ffn_w_down, 'final_norm': final_norm}


def reference(x_prompt, x_sample, state_mlstm_C, state_mlstm_n, state_mlstm_m, state_gdn_S, c, c_ctx,
              ada_w, ada_b, norm_mix, norm_ffn, w_in, hyena_conv_w, hyena_w1, hyena_b1, hyena_w2, hyena_b2,
              hyena_freq, hyena_w3, hyena_bias, mlstm_gate_b, mlstm_norm, gdn_conv_w, gdn_A_log, gdn_dt_bias,
              gdn_norm, w_branch, w_out, ffn_w_gate, ffn_w_up, ffn_conv_w, ffn_conv_b, ffn_w_down, final_norm):
    bp = x_prompt.shape[0]
    rows_lat = x_sample.shape[1] // GRID_W
    zero_states = (jnp.zeros((bp, 2, H_M, DK_M, DV_M), F32), jnp.zeros((bp, 2, H_M, DK_M), F32),
                   jnp.zeros((bp, 2, H_M), F32), jnp.zeros((bp, 2, H_G, DK_G, DV_G), F32))
    xp, xs = x_prompt, x_sample
    out_C, out_n, out_m, out_S = [], [], [], []
    for l in range(DEPTH):
        p = {'norm_mix': norm_mix[l], 'norm_ffn': norm_ffn[l], 'w_in': w_in[l],
             'hyena_conv_w': hyena_conv_w[l], 'hyena_w1': hyena_w1[l], 'hyena_b1': hyena_b1[l],
             'hyena_w2': hyena_w2[l], 'hyena_b2': hyena_b2[l], 'hyena_freq': hyena_freq[l],
             'hyena_w3': hyena_w3[l], 'hyena_bias': hyena_bias[l],
             'mlstm_gate_b': mlstm_gate_b[l], 'mlstm_norm': mlstm_norm[l],
             'gdn_conv_w': gdn_conv_w[l], 'gdn_A_log': gdn_A_log[l], 'gdn_dt_bias': gdn_dt_bias[l],
             'gdn_norm': gdn_norm[l], 'w_branch': w_branch[l], 'w_out': w_out[l],
             'ffn_w_gate': ffn_w_gate[l], 'ffn_w_up': ffn_w_up[l], 'ffn_conv_w': ffn_conv_w[l],
             'ffn_conv_b': ffn_conv_b[l], 'ffn_w_down': ffn_w_down[l]}
        xp, (sC, sn, sm, sS) = trunk_layer(xp, ada_mod(c_ctx, ada_w[l], ada_b[l]), p, 1, zero_states)
        out_C.append(sC)
        out_n.append(sn)
        out_m.append(sm)
        out_S.append(sS)
        cached = (state_mlstm_C[:, l].astype(F32), state_mlstm_n[:, l].astype(F32),
                  state_mlstm_m[:, l].astype(F32), state_gdn_S[:, l].astype(F32))
        xs, _ = trunk_layer(xs, ada_mod(c, ada_w[l], ada_b[l]), p, rows_lat, cached)
    y_prompt = rmsnorm(xp, final_norm)
    y_sample = rmsnorm(xs, final_norm)
    new_mlstm_C = jnp.stack(out_C, axis=1)
    new_mlstm_n = jnp.stack(out_n, axis=1)
    new_mlstm_m = jnp.stack(out_m, axis=1)
    new_gdn_S = jnp.stack(out_S, axis=1)
    return (y_prompt, y_sample, new_mlstm_C, new_mlstm_n, new_mlstm_m, new_gdn_S)
```

```python
import functools
import math

import jax
import jax.numpy as jnp
import numpy as np
from jax import lax
from jax.experimental import pallas as pl
from jax.experimental.pallas import tpu as pltpu

D_MODEL = 4096
DEPTH = 2
GRID_W = 64
N_BRANCH = 3
BRANCH_W = D_MODEL // 2
W_H = BRANCH_W
HYENA_ORDER = 2
FILTER_BANDS = 16
H_M = 8
DV_M = BRANCH_W // H_M
DK_M = DV_M // 2
H_G = 16
DK_G = BRANCH_W // H_G
DV_G = BRANCH_W // H_G
D_FF = 11008
D_FF_PAD = 11264
CHUNK = 64
EPS = 1e-6
F32 = jnp.float32
BF16 = jnp.bfloat16

_IN_SPLITS = (3 * W_H, H_M * DK_M, H_M * DK_M, H_M * DV_M, H_M * DV_M, 4 * H_M,
              H_G * (2 * DK_G + DV_G), H_G * DV_G, 2 * H_G, 2 * H_G, N_BRANCH * D_MODEL)
_OFF = np.concatenate([[0], np.cumsum(_IN_SPLITS)]).tolist()
N_SMALL_PAD = 128

VMEM_LIMIT_BYTES = 56 * 1024 * 1024


def _mm_kernel(x_ref, w_ref, o_ref, *, nk):
    part = jnp.dot(x_ref[...].astype(BF16), w_ref[...].astype(BF16), preferred_element_type=F32)
    if nk == 1:
        o_ref[...] = part
    else:
        k = pl.program_id(2)

        @pl.when(k == 0)
        def _():
            o_ref[...] = part

        @pl.when(k != 0)
        def _():
            o_ref[...] += part


def matmul(x, w, *, tm, tn, tk=None):
    M, K = x.shape
    K2, N = w.shape
    assert K == K2
    tk = K if tk is None else tk
    tm = min(tm, M)
    assert M % tm == 0 and N % tn == 0 and K % tk == 0, (M, N, K, tm, tn, tk)
    nk = K // tk
    return pl.pallas_call(
        functools.partial(_mm_kernel, nk=nk),
        grid=(M // tm, N // tn, nk),
        in_specs=[pl.BlockSpec((tm, tk), lambda i, j, k: (i, k)),
                  pl.BlockSpec((tk, tn), lambda i, j, k: (k, j))],
        out_specs=pl.BlockSpec((tm, tn), lambda i, j, k: (i, j)),
        out_shape=jax.ShapeDtypeStruct((M, N), F32),
        compiler_params=pltpu.CompilerParams(
            dimension_semantics=("parallel", "parallel", "arbitrary"),
            vmem_limit_bytes=VMEM_LIMIT_BYTES),
    )(x, w)


def rmsnorm(x, g):
    xf = x.astype(F32)
    y = xf * lax.rsqrt(jnp.mean(xf * xf, axis=-1, keepdims=True) + EPS)
    return (y * g.astype(F32)).astype(x.dtype)


def head_rmsnorm(h, g):
    return h * lax.rsqrt(jnp.mean(h * h, axis=-1, keepdims=True) + EPS) * g.astype(F32)


def l2norm(x):
    return x * lax.rsqrt(jnp.sum(x * x, axis=-1, keepdims=True) + EPS)


def conv1d_centred(x, w):
    xp = jnp.pad(x, ((0, 0), (1, 1), (0, 0)))
    return xp[:, :-2] * w[0] + xp[:, 1:-1] * w[1] + xp[:, 2:] * w[2]


def dwconv_grid(x, w, b, rows):
    B, L, C = x.shape
    y = lax.conv_general_dilated(x.reshape(B, rows, L // rows, C), w[:, :, None, :].astype(x.dtype),
                                 window_strides=(1, 1), padding='SAME',
                                 dimension_numbers=('NHWC', 'HWIO', 'NHWC'), feature_group_count=C)
    return y.reshape(B, L, C) + b.astype(x.dtype)


def to_chunks(a):
    B, L = a.shape[:2]
    return jnp.moveaxis(a.reshape((B, L // CHUNK, CHUNK) + a.shape[2:]), 1, 0)


def from_chunks(a):
    nc, B, T = a.shape[:3]
    return jnp.moveaxis(a, 0, 1).reshape((B, nc * T) + a.shape[3:])


def flip(a):
    return jnp.flip(a, axis=1)


def hyena_filters(L, w1, b1, w2, b2, freq, w3):
    t = jnp.arange(L, dtype=F32) / L
    bands = jnp.linspace(1e-4, FILTER_BANDS - 1, FILTER_BANDS, dtype=F32)
    ang = 2.0 * math.pi * t[:, None] * bands[None, :]
    feat = jnp.concatenate([t[:, None], jnp.cos(ang), jnp.sin(ang)], axis=-1)
    fr = freq.astype(F32)
    hid = jnp.sin(fr * (feat @ w1.astype(F32) + b1.astype(F32)))
    hid = jnp.sin(fr * (hid @ w2.astype(F32) + b2.astype(F32)))
    h = (hid @ w3.astype(F32)).reshape(L, HYENA_ORDER, 2, W_H)
    deltas = jnp.abs(jnp.linspace(math.log(1e-2) / 1.5, math.log(1e-2) / 0.3, W_H, dtype=F32))
    h = h * jnp.exp(-t[:, None] * deltas[None, :])[:, None, None, :]
    return h * lax.rsqrt(jnp.sum(h * h, axis=(0, 2), keepdims=True) + EPS)


def fft_longconv(u, k_fwd, k_bwd, bias):
    L, C = k_fwd.shape
    k = jnp.concatenate([k_fwd[:1] + k_bwd[:1], k_fwd[1:], jnp.zeros((1, C), F32), k_bwd[:0:-1]], axis=0)
    y = jnp.fft.irfft(jnp.fft.rfft(u, n=2 * L, axis=1) * jnp.fft.rfft(k, axis=0)[None], n=2 * L, axis=1)[:, :L]
    return y + u * bias.astype(F32)


def hyena_mixer(zh, conv_w, filt, bias):
    zc = conv1d_centred(zh, conv_w.astype(zh.dtype)).astype(F32)
    v, x1, x2 = jnp.split(zc, 3, axis=-1)
    s = x1 * fft_longconv(v, filt[:, 0, 0], filt[:, 0, 1], bias[0])
    return x2 * fft_longconv(s, filt[:, 1, 0], filt[:, 1, 1], bias[1])


def mlstm_scan(q, k, v, log_i, log_f, C0, n0, m0):
    causal = jnp.tril(jnp.ones((CHUNK, CHUNK), dtype=bool))[None, :, :, None]

    def step(carry, inp):
        C, n, m = carry
        qc, kc, vc, ic, fc = inp
        b = jnp.cumsum(fc, axis=1)
        dmat = jnp.where(causal, b[:, :, None, :] - b[:, None, :, :] + ic[:, None, :, :], -jnp.inf)
        inter = b + m[:, None, :]
        m_t = jnp.maximum(inter, jnp.max(dmat, axis=2))
        s = jnp.einsum('bthd,bshd->btsh', qc, kc) * jnp.exp(dmat - m_t[:, :, None, :])
        w_inter = jnp.exp(inter - m_t)
        num = jnp.einsum('btsh,bshv->bthv', s, vc) + w_inter[..., None] * jnp.einsum('bthd,bhdv->bthv', qc, C)
        den = jnp.sum(s, axis=2) + w_inter * jnp.einsum('bthd,bhd->bth', qc, n)
        h = num / jnp.maximum(jnp.abs(den), jnp.exp(-m_t))[..., None]
        b_end = b[:, -1]
        g = b_end[:, None, :] - b + ic
        m_new = jnp.maximum(b_end + m, jnp.max(g, axis=1))
        wg = jnp.exp(g - m_new[:, None, :])
        carry_decay = jnp.exp(b_end + m - m_new)
        C = carry_decay[..., None, None] * C + jnp.einsum('bsh,bshd,bshv->bhdv', wg, kc, vc)
        n = carry_decay[..., None] * n + jnp.einsum('bsh,bshd->bhd', wg, kc)
        return (C, n, m_new), h

    xs = tuple(to_chunks(a) for a in (q, k, v, log_i, log_f))
    state, h = lax.scan(step, (C0, n0, m0), xs)
    return from_chunks(h), state


def mlstm_mixer(mq, mk, mv, mo, mg, gate_b, norm_g, C0, n0, m0):
    B, L, _ = mq.shape
    q = mq.astype(F32).reshape(B, L, H_M, DK_M)
    k = mk.astype(F32).reshape(B, L, H_M, DK_M) * DK_M ** -0.5
    v = mv.astype(F32).reshape(B, L, H_M, DV_M)
    gp = mg.astype(F32).reshape(B, L, 4, H_M) + gate_b.astype(F32)
    log_i = gp[:, :, 0:2]
    log_f = jax.nn.log_sigmoid(gp[:, :, 2:4])
    hf, (Cf, nf, mf) = mlstm_scan(q, k, v, log_i[:, :, 0], log_f[:, :, 0], C0[:, 0], n0[:, 0], m0[:, 0])
    hb, (Cb, nb, mb) = mlstm_scan(flip(q), flip(k), flip(v), flip(log_i[:, :, 1]), flip(log_f[:, :, 1]),
                                  C0[:, 1], n0[:, 1], m0[:, 1])
    h = head_rmsnorm(hf + flip(hb), norm_g)
    y = h.reshape(B, L, H_M * DV_M) * jax.nn.sigmoid(mo.astype(F32))
    return y, (jnp.stack([Cf, Cb], axis=1), jnp.stack([nf, nb], axis=1), jnp.stack([mf, mb], axis=1))


def gdn_scan(q, k, v, g, beta, S0):
    tri = jnp.tril(jnp.ones((CHUNK, CHUNK), dtype=bool))
    strict = jnp.tril(jnp.ones((CHUNK, CHUNK), dtype=bool), k=-1)
    eye = jnp.eye(CHUNK, dtype=F32)

    def step(S, inp):
        qc, kc, vc, gc, bc = inp
        G = jnp.swapaxes(jnp.cumsum(gc, axis=1), 1, 2)
        bh = jnp.swapaxes(bc, 1, 2)
        dec = jnp.exp(jnp.where(tri, G[..., :, None] - G[..., None, :], -jnp.inf))
        kk = jnp.einsum('bthd,bshd->bhts', kc, kc)
        a = eye + jnp.where(strict, bh[..., :, None] * kk * dec, 0.0)
        eG = jnp.exp(G)
        rhs = jnp.concatenate([(bh * eG)[..., None] * jnp.swapaxes(kc, 1, 2),
                               bh[..., None] * jnp.swapaxes(vc, 1, 2)], axis=-1)
        wu = lax.linalg.triangular_solve(a, rhs, left_side=True, lower=True, unit_diagonal=True)
        w, u = wu[..., :DK_G], wu[..., DK_G:]
        v_new = u - jnp.einsum('bhtk,bhkv->bhtv', w, S)
        qk = jnp.einsum('bthk,bshk->bhts', qc, kc) * dec
        o = eG[..., None] * jnp.einsum('bthk,bhkv->bhtv', qc, S) + jnp.einsum('bhts,bhsv->bhtv', qk, v_new)
        G_end = G[..., -1]
        S = jnp.exp(G_end)[..., None, None] * S + jnp.einsum('bhs,bshk,bhsv->bhkv',
                                                             jnp.exp(G_end[..., None] - G), kc, v_new)
        return S, jnp.swapaxes(o, 1, 2)

    xs = tuple(to_chunks(a) for a in (q, k, v, g, beta))
    S, o = lax.scan(step, S0, xs)
    return from_chunks(o), S


def gdn_mixer(gqkv, gz, gb, ga, conv_w, A_log, dt_bias, norm_g, S0):
    B, L, _ = gqkv.shape
    qkv = jax.nn.silu(conv1d_centred(gqkv, conv_w.astype(gqkv.dtype)).astype(F32))
    q, k, v = jnp.split(qkv, [H_G * DK_G, 2 * H_G * DK_G], axis=-1)
    q = l2norm(q.reshape(B, L, H_G, DK_G)) * DK_G ** -0.5
    k = l2norm(k.reshape(B, L, H_G, DK_G))
    v = v.reshape(B, L, H_G, DV_G)
    beta = jax.nn.sigmoid(gb.astype(F32).reshape(B, L, 2, H_G))
    g = -jnp.exp(A_log.astype(F32)) * jax.nn.softplus(ga.astype(F32).reshape(B, L, 2, H_G) + dt_bias.astype(F32))
    of, Sf = gdn_scan(q, k, v, g[:, :, 0], beta[:, :, 0], S0[:, 0])
    ob, Sb = gdn_scan(flip(q), flip(k), flip(v), flip(g[:, :, 1]), flip(beta[:, :, 1]), S0[:, 1])
    o = head_rmsnorm(of + flip(ob), norm_g)
    y = o.reshape(B, L, H_G * DV_G) * jax.nn.silu(gz.astype(F32))
    return y, jnp.stack([Sf, Sb], axis=1)


def mixing_block(h, p, states):
    B, L, _ = h.shape
    mC, mn, mm, gS = states
    hb = h.reshape(B * L, D_MODEL).astype(BF16)
    z = matmul(hb, p['w_in_main'], tm=1024, tn=512).reshape(B, L, -1)
    zs = matmul(hb, p['w_in_small'], tm=1024, tn=N_SMALL_PAD).reshape(B, L, -1)
    o = 0
    parts = []
    for n in (3 * W_H, H_M * DK_M, H_M * DK_M, H_M * DV_M, H_M * DV_M,
              H_G * (2 * DK_G + DV_G), H_G * DV_G, N_BRANCH * D_MODEL):
        parts.append(z[..., o:o + n])
        o += n
    zh, mq, mk, mv, mo, gqkv, gz, zm = parts
    mg, gb, ga = zs[..., 0:32], zs[..., 32:64], zs[..., 64:96]
    filt = hyena_filters(L, p['hyena_w1'], p['hyena_b1'], p['hyena_w2'], p['hyena_b2'], p['hyena_freq'], p['hyena_w3'])
    y_a = hyena_mixer(zh, p['hyena_conv_w'], filt, p['hyena_bias'])
    y_b, m_state = mlstm_mixer(mq, mk, mv, mo, mg, p['mlstm_gate_b'], p['mlstm_norm'], mC, mn, mm)
    y_c, g_state = gdn_mixer(gqkv, gz, gb, ga, p['gdn_conv_w'], p['gdn_A_log'], p['gdn_dt_bias'], p['gdn_norm'], gS)
    gates = jax.nn.sigmoid(zm).reshape(B, L, N_BRANCH, D_MODEL)
    wb = p['w_branch']
    mix = 0.0
    for i, y in enumerate((y_a, y_b, y_c)):
        pr = matmul(y.reshape(B * L, BRANCH_W).astype(BF16), wb[i], tm=1024, tn=512).reshape(B, L, D_MODEL)
        mix = mix + gates[:, :, i] * pr
    out = matmul(mix.reshape(B * L, D_MODEL).astype(BF16), p['w_out'], tm=1024, tn=512)
    return out.reshape(B, L, D_MODEL), m_state + (g_state,)


def trunk_layer(x, mod, p, rows, states):
    B, L, _ = x.shape
    sh1, sc1, g1, sh2, sc2, g2 = mod
    h = rmsnorm(x, p['norm_mix']) * (1.0 + sc1) + sh1
    mix, new_states = mixing_block(h, p, states)
    x = x + g1 * mix
    h = rmsnorm(x, p['norm_ffn']) * (1.0 + sc2) + sh2
    hb = h.reshape(B * L, D_MODEL).astype(BF16)
    gate_pre = matmul(hb, p['ffn_w_gate'], tm=1024, tn=1024).reshape(B, L, D_FF_PAD)
    up = matmul(hb, p['ffn_w_up'], tm=1024, tn=1024).reshape(B, L, D_FF_PAD)
    gate = dwconv_grid(gate_pre, p['ffn_conv_w'], p['ffn_conv_b'], rows)
    act = (jax.nn.silu(gate) * up).reshape(B * L, D_FF_PAD).astype(BF16)
    down = matmul(act, p['ffn_w_down'], tm=1024, tn=512, tk=D_FF_PAD // 2).reshape(B, L, D_MODEL)
    x = x + g2 * down
    return x, new_states


def _ada_all(c, c_ctx, ada_w, ada_b):
    cc = jnp.concatenate([c, c_ctx[None, :], jnp.zeros((7, D_MODEL), F32)], axis=0)
    a = jax.nn.silu(cc)
    outs = []
    for l in range(DEPTH):
        outs.append(matmul(a, ada_w[l], tm=16, tn=1024) + ada_b[l])
    return outs


def kernel(x_prompt, x_sample, state_mlstm_C, state_mlstm_n, state_mlstm_m, state_gdn_S, c, c_ctx,
           ada_w, ada_b, norm_mix, norm_ffn, w_in, hyena_conv_w, hyena_w1, hyena_b1, hyena_w2, hyena_b2,
           hyena_freq, hyena_w3, hyena_bias, mlstm_gate_b, mlstm_norm, gdn_conv_w, gdn_A_log, gdn_dt_bias,
           gdn_norm, w_branch, w_out, ffn_w_gate, ffn_w_up, ffn_conv_w, ffn_conv_b, ffn_w_down, final_norm):
    bp = x_prompt.shape[0]
    nb = x_sample.shape[0]
    rows_lat = x_sample.shape[1] // GRID_W
    zero_states = (jnp.zeros((bp, 2, H_M, DK_M, DV_M), F32), jnp.zeros((bp, 2, H_M, DK_M), F32),
                   jnp.zeros((bp, 2, H_M), F32), jnp.zeros((bp, 2, H_G, DK_G, DV_G), F32))
    ada = _ada_all(c, c_ctx, ada_w, ada_b)
    ffpad = D_FF_PAD - D_FF
    xp, xs = x_prompt, x_sample
    out_C, out_n, out_m, out_S = [], [], [], []
    for l in range(DEPTH):
        wl = w_in[l]
        w_main = jnp.concatenate([wl[:, :_OFF[5]], wl[:, _OFF[6]:_OFF[8]], wl[:, _OFF[10]:]], axis=1).astype(BF16)
        w_small = jnp.concatenate([wl[:, _OFF[5]:_OFF[6]], wl[:, _OFF[8]:_OFF[10]],
                                   jnp.zeros((D_MODEL, N_SMALL_PAD - 96), F32)], axis=1).astype(BF16)
        p = {'norm_mix': norm_mix[l], 'norm_ffn': norm_ffn[l], 'w_in_main': w_main, 'w_in_small': w_small,
             'hyena_conv_w': hyena_conv_w[l], 'hyena_w1': hyena_w1[l], 'hyena_b1': hyena_b1[l],
             'hyena_w2': hyena_w2[l], 'hyena_b2': hyena_b2[l], 'hyena_freq': hyena_freq[l],
             'hyena_w3': hyena_w3[l], 'hyena_bias': hyena_bias[l],
             'mlstm_gate_b': mlstm_gate_b[l], 'mlstm_norm': mlstm_norm[l],
             'gdn_conv_w': gdn_conv_w[l], 'gdn_A_log': gdn_A_log[l], 'gdn_dt_bias': gdn_dt_bias[l],
             'gdn_norm': gdn_norm[l], 'w_branch': w_branch[l].astype(BF16), 'w_out': w_out[l].astype(BF16),
             'ffn_w_gate': jnp.pad(ffn_w_gate[l], ((0, 0), (0, ffpad))).astype(BF16),
             'ffn_w_up': jnp.pad(ffn_w_up[l], ((0, 0), (0, ffpad))).astype(BF16),
             'ffn_conv_w': jnp.pad(ffn_conv_w[l], ((0, 0), (0, 0), (0, ffpad))),
             'ffn_conv_b': jnp.pad(ffn_conv_b[l], ((0, ffpad),)),
             'ffn_w_down': jnp.pad(ffn_w_down[l], ((0, ffpad), (0, 0))).astype(BF16)}
        m = ada[l]
        mod_ctx = [t[None, None, :] for t in jnp.split(m[nb], 6, axis=-1)]
        mod_lat = [t[:, None, :] for t in jnp.split(m[:nb], 6, axis=-1)]
        xp, (sC, sn, sm, sS) = trunk_layer(xp, mod_ctx, p, 1, zero_states)
        out_C.append(sC)
        out_n.append(sn)
        out_m.append(sm)
        out_S.append(sS)
        cached = (state_mlstm_C[:, l].astype(F32), state_mlstm_n[:, l].astype(F32),
                  state_mlstm_m[:, l].astype(F32), state_gdn_S[:, l].astype(F32))
        xs, _ = trunk_layer(xs, mod_lat, p, rows_lat, cached)
    y_prompt = rmsnorm(xp, final_norm)
    y_sample = rmsnorm(xs, final_norm)
    return (y_prompt, y_sample, jnp.stack(out_C, axis=1), jnp.stack(out_n, axis=1),
            jnp.stack(out_m, axis=1), jnp.stack(out_S, axis=1))
```

```python
import functools
import math

import jax
import jax.numpy as jnp
import numpy as np
from jax import lax
from jax.experimental import pallas as pl
from jax.experimental.pallas import tpu as pltpu

D_MODEL = 4096
DEPTH = 2
GRID_W = 64
N_BRANCH = 3
BRANCH_W = D_MODEL // 2
W_H = BRANCH_W
HYENA_ORDER = 2
FILTER_BANDS = 16
H_M = 8
DV_M = BRANCH_W // H_M
DK_M = DV_M // 2
H_G = 16
DK_G = BRANCH_W // H_G
DV_G = BRANCH_W // H_G
D_FF = 11008
D_FF_PAD = 11264
CHUNK = 64
EPS = 1e-6
F32 = jnp.float32
BF16 = jnp.bfloat16

_IN_SPLITS = (3 * W_H, H_M * DK_M, H_M * DK_M, H_M * DV_M, H_M * DV_M, 4 * H_M,
              H_G * (2 * DK_G + DV_G), H_G * DV_G, 2 * H_G, 2 * H_G, N_BRANCH * D_MODEL)
_OFF = np.concatenate([[0], np.cumsum(_IN_SPLITS)]).tolist()
N_SMALL_PAD = 128

VMEM_LIMIT_BYTES = 56 * 1024 * 1024


def _mm_kernel(x_ref, w_ref, o_ref, *, nk):
    part = jnp.dot(x_ref[...].astype(BF16), w_ref[...].astype(BF16), preferred_element_type=F32)
    if nk == 1:
        o_ref[...] = part
    else:
        k = pl.program_id(2)

        @pl.when(k == 0)
        def _():
            o_ref[...] = part

        @pl.when(k != 0)
        def _():
            o_ref[...] += part


def matmul(x, w, *, tm, tn, tk=None):
    M, K = x.shape
    K2, N = w.shape
    assert K == K2
    tk = K if tk is None else tk
    tm = min(tm, M)
    assert M % tm == 0 and N % tn == 0 and K % tk == 0, (M, N, K, tm, tn, tk)
    nk = K // tk
    return pl.pallas_call(
        functools.partial(_mm_kernel, nk=nk),
        grid=(M // tm, N // tn, nk),
        in_specs=[pl.BlockSpec((tm, tk), lambda i, j, k: (i, k)),
                  pl.BlockSpec((tk, tn), lambda i, j, k: (k, j))],
        out_specs=pl.BlockSpec((tm, tn), lambda i, j, k: (i, j)),
        out_shape=jax.ShapeDtypeStruct((M, N), F32),
        compiler_params=pltpu.CompilerParams(
            dimension_semantics=("parallel", "parallel", "arbitrary"),
            vmem_limit_bytes=VMEM_LIMIT_BYTES),
    )(x, w)


def _dot(a, b):
    return jnp.dot(a.astype(BF16), b.astype(BF16), preferred_element_type=F32)


def _dot_nt(a, b):
    return lax.dot_general(a.astype(BF16), b.astype(BF16), (((1,), (1,)), ((), ())), preferred_element_type=F32)


def _dot_tn(a, b):
    return lax.dot_general(a.astype(BF16), b.astype(BF16), (((0,), (0,)), ((), ())), preferred_element_type=F32)


def _split2(a):
    hi = a.astype(BF16)
    return hi, (a - hi.astype(F32)).astype(BF16)


def _split3(a):
    hi = a.astype(BF16)
    r = a - hi.astype(F32)
    mid = r.astype(BF16)
    return hi, mid, (r - mid.astype(F32)).astype(BF16)


def _dot3(a, b):
    ah, al = _split2(a)
    bh, bl = _split2(b)
    return (jnp.dot(ah, bl, preferred_element_type=F32) + jnp.dot(al, bh, preferred_element_type=F32)
            + jnp.dot(ah, bh, preferred_element_type=F32))


def _dot_exact_lhs(a_bf16, b):
    b0, b1, b2 = _split3(b)
    return (jnp.dot(a_bf16, b2, preferred_element_type=F32) + jnp.dot(a_bf16, b1, preferred_element_type=F32)
            + jnp.dot(a_bf16, b0, preferred_element_type=F32))


def _inv_unit_triangular(ns, eye):
    xs = [eye - n for n in ns]
    ps = list(ns)
    for _ in range(5):
        ps = [_dot(p, p) for p in ps]
        xs = [x + _dot(x, p) for x, p in zip(xs, ps)]
    rs = [eye - x - _dot3(n, x) for n, x in zip(ns, xs)]
    return [x + _dot(x, r) for x, r in zip(xs, rs)]


def _softplus(x):
    return jnp.maximum(x, 0.0) + jnp.log1p(jnp.exp(-jnp.abs(x)))


def _lane_pick(x, lane):
    idx = lax.broadcasted_iota(jnp.int32, x.shape, 1)
    return jnp.sum(jnp.where(idx == lane, x, 0.0), axis=-1, keepdims=True)


def _conv3_rows(ref, w_ref, r0, nrows, total):
    x = ref[pl.ds(r0, nrows), :]
    row = lax.broadcasted_iota(jnp.int32, x.shape, 0)
    p0 = jnp.maximum(r0 - 8, 0)
    n0 = jnp.minimum(r0 + nrows, total - 8)
    prev = ref[pl.ds(pl.multiple_of(p0, 8), 8), :][7:8, :] * (r0 > 0).astype(F32)
    nxt = ref[pl.ds(pl.multiple_of(n0, 8), 8), :][0:1, :] * (r0 + nrows < total).astype(F32)
    xm1 = jnp.where(row == 0, prev, pltpu.roll(x, 1, 0))
    xp1 = jnp.where(row == nrows - 1, nxt, pltpu.roll(x, nrows - 1, 0))
    return xm1 * w_ref[0:1, :] + x * w_ref[1:2, :] + xp1 * w_ref[2:3, :]


GDN_LANE_BETA = 32
GDN_LANE_DECAY = 64
PREP_ROWS = 256
GDN_GROUP = 4


def _gdn_kernel(zq_ref, zk_ref, zv_ref, gz_ref, zs_ref, cwq_ref, cwk_ref, cwv_ref, alog_ref, dtb_ref, ng_ref,
                s0_ref, y_ref, sout_ref,
                q_s, k_s, v_s, gall_s, ball_s, o_s, wq_s, u_s, qkd_s, kd_s, eend_s, st_s, *, L, zero_init):
    h = pl.program_id(1)
    nc = L // CHUNK
    T = CHUNK
    row = lax.broadcasted_iota(jnp.int32, (T, T), 0)
    col = lax.broadcasted_iota(jnp.int32, (T, T), 1)
    eye = (row == col).astype(F32)
    ones_b = jnp.ones((T, T), BF16)
    incl = (row >= col, row <= col)
    strict = (row > col, row < col)

    @pl.when(h == 0)
    def _():
        lane = lax.broadcasted_iota(jnp.int32, (T, 128), 1)
        tril_b = incl[0].astype(BF16)
        triu_b = incl[1].astype(BF16)

        def seg(c, carry):
            r = pl.ds(pl.multiple_of(c * T, T), T)
            zs = zs_ref[r, :]
            ball_s[r, :] = jax.nn.sigmoid(zs)
            g = -jnp.exp(alog_ref[...]) * _softplus(zs + dtb_ref[...])
            pre = _dot_exact_lhs(tril_b, g)
            suf = _dot_exact_lhs(triu_b, g)
            gall_s[r, :] = jnp.where(lane < GDN_LANE_DECAY + H_G, pre, suf)
            return carry

        lax.fori_loop(0, nc, seg, 0)

    def prep(i, carry):
        r0 = pl.multiple_of(i * PREP_ROWS, PREP_ROWS)
        r = pl.ds(r0, PREP_ROWS)
        q = _conv3_rows(zq_ref, cwq_ref, r0, PREP_ROWS, L)
        q = q * jax.nn.sigmoid(q)
        q = q * lax.rsqrt(jnp.sum(q * q, axis=-1, keepdims=True) + EPS) * DK_G ** -0.5
        q_s[r, :] = q.astype(BF16)
        o_s[r, :] = jnp.zeros((PREP_ROWS, 128), F32)
        k = _conv3_rows(zk_ref, cwk_ref, r0, PREP_ROWS, L)
        k = k * jax.nn.sigmoid(k)
        k_s[r, :] = k * lax.rsqrt(jnp.sum(k * k, axis=-1, keepdims=True) + EPS)
        v = _conv3_rows(zv_ref, cwv_ref, r0, PREP_ROWS, L)
        v_s[r, :] = v * jax.nn.sigmoid(v)
        return carry

    lax.fori_loop(0, L // PREP_ROWS, prep, 0)

    def phase_a(gi, carry):
        cs = [gi * GDN_GROUP + j for j in range(GDN_GROUP)]
        rs = [pl.ds(pl.multiple_of(c * T, T), T) for c in cs]
        qs = [q_s[r, :] for r in rs]
        ks = [k_s[r, :] for r in rs]
        vs = [v_s[r, :] for r in rs]
        kks = [_dot_nt(k, k) for k in ks]
        qks = [_dot_nt(q, k) for q, k in zip(qs, ks)]
        items = [(j, d) for j in range(GDN_GROUP) for d in range(2)]
        gcs = [_lane_pick(gall_s[rs[j], :], GDN_LANE_DECAY + d * H_G + h) for j, d in items]
        bcs = [_lane_pick(ball_s[rs[j], :], GDN_LANE_BETA + d * H_G + h) for j, d in items]
        gsq = [jnp.broadcast_to(g, (T, T)) for g in gcs]
        grs = [_dot_exact_lhs(ones_b, jnp.where(row == col, g, 0.0)) for g in gsq]
        decs = [jnp.exp(jnp.where(incl[d], g - gr, -1e30)) for (j, d), g, gr in zip(items, gsq, grs)]
        ns = [jnp.where(strict[d], b * kks[j] * dec, 0.0) for (j, d), b, dec in zip(items, bcs, decs)]
        tinvs = _inv_unit_triangular(ns, eye)
        egs = [jnp.exp(g) for g in gcs]
        ws = [_dot3(t, (b * eg) * ks[j]) for (j, d), t, b, eg in zip(items, tinvs, bcs, egs)]
        us = [_dot3(t, b * vs[j]) for (j, d), t, b in zip(items, tinvs, bcs)]
        for idx, (j, d) in enumerate(items):
            c, r, gc = cs[j], rs[j], gcs[idx]
            g_end = gc[T - 1:T, :] if d == 0 else gc[0:1, :]
            wq_s[d, pl.ds(pl.multiple_of(c * 2 * T, 2 * T), T), :] = ws[idx].astype(BF16)
            wq_s[d, pl.ds(pl.multiple_of(c * 2 * T, 2 * T) + T, T), :] = (egs[idx] * qs[j]).astype(BF16)
            u_s[d, r, :] = us[idx]
            qkd_s[d, r, :] = (qks[j] * decs[idx]).astype(BF16)
            kd_s[d, r, :] = (jnp.exp(g_end - gc) * ks[j]).astype(BF16)
            eend_s[d, pl.ds(pl.multiple_of(c * 8, 8), 8), :] = jnp.broadcast_to(jnp.exp(g_end), (8, 128))
        return carry

    lax.fori_loop(0, nc // GDN_GROUP, phase_a, 0)

    for d in range(2):
        if zero_init:
            st_s[d] = jnp.zeros((DK_G, DV_G), F32)
        else:
            st_s[d] = s0_ref[0, d, 0]

    def phase_b(i, carry):
        cs = (i, nc - 1 - i)
        rs = [pl.ds(pl.multiple_of(c * T, T), T) for c in cs]
        ss = [st_s[d] for d in range(2)]
        wss = [jnp.dot(wq_s[d, pl.ds(pl.multiple_of(cs[d] * 2 * T, 2 * T), 2 * T), :], ss[d].astype(BF16),
                       preferred_element_type=F32) for d in range(2)]
        vns = [(u_s[d, rs[d], :] - wss[d][:T]).astype(BF16) for d in range(2)]
        os_ = [jnp.dot(qkd_s[d, rs[d], :], vns[d], preferred_element_type=F32) for d in range(2)]
        kvs = [_dot_tn(kd_s[d, rs[d], :], vns[d]) for d in range(2)]
        for d in range(2):
            e_end = eend_s[d, pl.ds(pl.multiple_of(cs[d] * 8, 8), 8), :][0:1, :]
            st_s[d] = e_end * ss[d] + kvs[d]
        for d in range(2):
            o_s[rs[d], :] += wss[d][T:] + os_[d]
        return carry

    lax.fori_loop(0, nc, phase_b, 0)
    for d in range(2):
        sout_ref[0, d, 0] = st_s[d]

    def fin(i, carry):
        r = pl.ds(pl.multiple_of(i * PREP_ROWS, PREP_ROWS), PREP_ROWS)
        o = o_s[r, :]
        o = o * lax.rsqrt(jnp.mean(o * o, axis=-1, keepdims=True) + EPS) * ng_ref[...]
        z = gz_ref[r, :]
        y_ref[r, :] = (o * (z * jax.nn.sigmoid(z))).astype(y_ref.dtype)
        return carry

    lax.fori_loop(0, L // PREP_ROWS, fin, 0)


Z_COL_GQ = (3 * W_H + 2 * H_M * DK_M + 2 * H_M * DV_M) // 128
Z_COL_GZ = Z_COL_GQ + 3 * H_G


def gdn_pallas(z, zs, conv_w, a_log, dt_bias, norm_g, s0, B, L):
    zero_init = s0 is None
    if zero_init:
        s0 = jnp.zeros((1, 2, 1, DK_G, DV_G), F32)
        s0_map = lambda b, h: (0, 0, 0, 0, 0)
    else:
        s0_map = lambda b, h: (b, 0, h, 0, 0)
    lanes = jnp.zeros((1, 128), F32)
    alog_row = lanes.at[0, GDN_LANE_DECAY:GDN_LANE_DECAY + 2 * H_G].set(a_log.reshape(-1))
    dtb_row = lanes.at[0, GDN_LANE_DECAY:GDN_LANE_DECAY + 2 * H_G].set(dt_bias.reshape(-1))
    nc = L // CHUNK
    col = lambda off: pl.BlockSpec((L, 128), lambda b, h: (b, off + h))
    cw = lambda off: pl.BlockSpec((3, 128), lambda b, h: (0, off + h))
    one = pl.BlockSpec((1, 128), lambda b, h: (0, 0))
    st = lambda m: pl.BlockSpec((1, 2, 1, DK_G, DV_G), m)
    return pl.pallas_call(
        functools.partial(_gdn_kernel, L=L, zero_init=zero_init),
        grid=(B, H_G),
        in_specs=[col(Z_COL_GQ), col(Z_COL_GQ + H_G), col(Z_COL_GQ + 2 * H_G), col(Z_COL_GZ),
                  pl.BlockSpec((L, 128), lambda b, h: (b, 0)),
                  cw(0), cw(H_G), cw(2 * H_G), one, one, one, st(s0_map)],
        out_specs=[pl.BlockSpec((L, 128), lambda b, h: (b, h)), st(lambda b, h: (b, 0, h, 0, 0))],
        out_shape=[jax.ShapeDtypeStruct((B * L, H_G * DV_G), BF16),
                   jax.ShapeDtypeStruct((B, 2, H_G, DK_G, DV_G), F32)],
        scratch_shapes=[pltpu.VMEM((L, 128), BF16), pltpu.VMEM((L, 128), F32), pltpu.VMEM((L, 128), F32),
                        pltpu.VMEM((L, 128), F32), pltpu.VMEM((L, 128), F32), pltpu.VMEM((L, 128), F32),
                        pltpu.VMEM((2, 2 * L, 128), BF16), pltpu.VMEM((2, L, 128), F32),
                        pltpu.VMEM((2, L, CHUNK), BF16), pltpu.VMEM((2, L, 128), BF16),
                        pltpu.VMEM((2, nc * 8, 128), F32), pltpu.VMEM((2, DK_G, DV_G), F32)],
        compiler_params=pltpu.CompilerParams(
            dimension_semantics=("parallel", "arbitrary"), vmem_limit_bytes=VMEM_LIMIT_BYTES),
    )(z, z, z, z, zs, conv_w, conv_w, conv_w, alog_row, dtb_row, norm_g.reshape(1, DV_G), s0)


MLSTM_LANE_I = 0
MLSTM_LANE_F = 2 * H_M
NEG_BIG = -1e30


def _mlstm_kernel(q_ref, k_ref, v_ref, mo_ref, zs_ref, gb_ref, ng_ref, c0_ref, n0_ref, m0_ref,
                  y_ref, cout_ref, nout_ref, mout_ref, gall_s, h_s, c_s, n_s, m_s, *, L, zero_init):
    h = pl.program_id(1)
    nc = L // CHUNK
    T = CHUNK
    row = lax.broadcasted_iota(jnp.int32, (T, T), 0)
    col = lax.broadcasted_iota(jnp.int32, (T, T), 1)
    ones_b = jnp.ones((T, T), BF16)
    incl = (row >= col, row <= col)

    @pl.when(h == 0)
    def _():
        lane = lax.broadcasted_iota(jnp.int32, (T, 128), 1)
        tril_b = incl[0].astype(BF16)
        triu_b = incl[1].astype(BF16)

        def seg(c, carry):
            r = pl.ds(pl.multiple_of(c * T, T), T)
            gp = zs_ref[r, :] + gb_ref[...]
            lf = jnp.minimum(gp, 0.0) - jnp.log1p(jnp.exp(-jnp.abs(gp)))
            pre = _dot_exact_lhs(tril_b, lf)
            suf = _dot_exact_lhs(triu_b, lf)
            gall_s[r, :] = jnp.where(lane < MLSTM_LANE_F, gp, jnp.where(lane < MLSTM_LANE_F + H_M, pre, suf))
            return carry

        lax.fori_loop(0, nc, seg, 0)

    for d in range(2):
        if zero_init:
            c_s[d] = jnp.zeros((DK_M, DV_M), F32)
            n_s[d] = jnp.zeros((8, DK_M), F32)
            m_s[d] = jnp.zeros((8, 128), F32)
        else:
            c_s[d] = c0_ref[0, d, 0]
            n_s[d] = jnp.broadcast_to(n0_ref[0, d, 0], (8, DK_M))
            m_s[d] = jnp.broadcast_to(m0_ref[0, d, 0], (8, 128))

    def zero_h(i, carry):
        h_s[pl.ds(pl.multiple_of(i * PREP_ROWS, PREP_ROWS), PREP_ROWS), :] = jnp.zeros((PREP_ROWS, DV_M), F32)
        return carry

    lax.fori_loop(0, L // PREP_ROWS, zero_h, 0)

    def step(i, carry):
        D = range(2)
        cs = (i, nc - 1 - i)
        rs = [pl.ds(pl.multiple_of(c * T, T), T) for c in cs]
        qs = [q_ref[r, :] for r in rs]
        ks = [k_ref[r, :] * DK_M ** -0.5 for r in rs]
        vs = [v_ref[r, :].astype(BF16) for r in rs]
        qbs = [q.astype(BF16) for q in qs]
        qks = [_dot_nt(qbs[d], ks[d]) for d in D]
        cst = [c_s[d] for d in D]
        qcs = [jnp.dot(qbs[d], cst[d].astype(BF16), preferred_element_type=F32) for d in D]
        bcol = [_lane_pick(gall_s[rs[d], :], MLSTM_LANE_F + d * H_M + h) for d in D]
        icol = [_lane_pick(gall_s[rs[d], :], MLSTM_LANE_I + d * H_M + h) for d in D]
        amb = [icol[d] - bcol[d] for d in D]
        rmat = [_dot_exact_lhs(ones_b, jnp.where(row == col, jnp.broadcast_to(amb[d], (T, T)), 0.0)) for d in D]
        dmat = [jnp.where(incl[d], bcol[d] + rmat[d], NEG_BIG) for d in D]
        m = [m_s[d][0:1, 0:1] for d in D]
        inter = [bcol[d] + m[d] for d in D]
        m_t = [jnp.maximum(inter[d], jnp.max(dmat[d], axis=-1, keepdims=True)) for d in D]
        smat = [qks[d] * jnp.exp(dmat[d] - m_t[d]) for d in D]
        w_inter = [jnp.exp(inter[d] - m_t[d]) for d in D]
        sv = [jnp.dot(smat[d].astype(BF16), vs[d], preferred_element_type=F32) for d in D]
        b_end = [bcol[0][T - 1:T, :], bcol[1][0:1, :]]
        gcol = [b_end[d] + amb[d] for d in D]
        m_new = [jnp.maximum(b_end[d] + m[d], jnp.max(gcol[d], axis=0, keepdims=True)) for d in D]
        kw = [jnp.exp(gcol[d] - m_new[d]) * ks[d] for d in D]
        kv = [_dot_tn(kw[d], vs[d]) for d in D]
        for d in D:
            n = n_s[d][0:1, :]
            cd = jnp.exp(b_end[d] + m[d] - m_new[d])
            num = sv[d] + w_inter[d] * qcs[d]
            den = (jnp.sum(smat[d], axis=-1, keepdims=True)
                   + w_inter[d] * jnp.sum(qs[d] * n, axis=-1, keepdims=True))
            h_s[rs[d], :] += num / jnp.maximum(jnp.abs(den), jnp.exp(-m_t[d]))
            c_s[d] = cd * cst[d] + kv[d]
            n_s[d] = jnp.broadcast_to(cd * n + jnp.sum(kw[d], axis=0, keepdims=True), (8, DK_M))
            m_s[d] = jnp.broadcast_to(m_new[d], (8, 128))
        return carry

    lax.fori_loop(0, nc, step, 0)
    for d in range(2):
        cout_ref[0, d, 0] = c_s[d]
        nout_ref[0, d, 0] = n_s[d][0:1, :]
        mout_ref[0, d, 0] = m_s[d][0:1, 0:1]

    def fin(i, carry):
        r = pl.ds(pl.multiple_of(i * PREP_ROWS, PREP_ROWS), PREP_ROWS)
        o = h_s[r, :]
        o = o * lax.rsqrt(jnp.mean(o * o, axis=-1, keepdims=True) + EPS) * ng_ref[0]
        y_ref[r, :] = (o * jax.nn.sigmoid(mo_ref[r, :])).astype(y_ref.dtype)
        return carry

    lax.fori_loop(0, L // PREP_ROWS, fin, 0)


Z_COL_MQ = 3 * W_H // 128
Z_COL_MV = (3 * W_H + 2 * H_M * DK_M) // DV_M


def mlstm_pallas(z, zs, gate_b, norm_g, state0, B, L):
    zero_init = state0 is None
    if zero_init:
        c0 = jnp.zeros((1, 2, 1, DK_M, DV_M), F32)
        n0 = jnp.zeros((1, 2, 1, 1, DK_M), F32)
        m0 = jnp.zeros((1, 2, 1, 1, 1), F32)
        smap = lambda b, h: (0, 0, 0, 0, 0)
    else:
        c0 = state0[0]
        n0 = state0[1].reshape(B, 2, H_M, 1, DK_M)
        m0 = state0[2].reshape(B, 2, H_M, 1, 1)
        smap = lambda b, h: (b, 0, h, 0, 0)
    omap = lambda b, h: (b, 0, h, 0, 0)
    gb_row = jnp.zeros((1, 128), F32).at[0, :4 * H_M].set(gate_b.reshape(-1))
    st_specs = lambda m: [pl.BlockSpec((1, 2, 1, DK_M, DV_M), m), pl.BlockSpec((1, 2, 1, 1, DK_M), m),
                          pl.BlockSpec((1, 2, 1, 1, 1), m)]
    y, c, n, m = pl.pallas_call(
        functools.partial(_mlstm_kernel, L=L, zero_init=zero_init),
        grid=(B, H_M),
        in_specs=[pl.BlockSpec((L, DK_M), lambda b, h: (b, Z_COL_MQ + h)),
                  pl.BlockSpec((L, DK_M), lambda b, h: (b, Z_COL_MQ + H_M + h)),
                  pl.BlockSpec((L, DV_M), lambda b, h: (b, Z_COL_MV + h)),
                  pl.BlockSpec((L, DV_M), lambda b, h: (b, Z_COL_MV + H_M + h)),
                  pl.BlockSpec((L, 128), lambda b, h: (b, 0)),
                  pl.BlockSpec((1, 128), lambda b, h: (0, 0)),
                  pl.BlockSpec((1, 1, DV_M), lambda b, h: (h, 0, 0))] + st_specs(smap),
        out_specs=[pl.BlockSpec((L, DV_M), lambda b, h: (b, h))] + st_specs(omap),
        out_shape=[jax.ShapeDtypeStruct((B * L, H_M * DV_M), BF16),
                   jax.ShapeDtypeStruct((B, 2, H_M, DK_M, DV_M), F32),
                   jax.ShapeDtypeStruct((B, 2, H_M, 1, DK_M), F32),
                   jax.ShapeDtypeStruct((B, 2, H_M, 1, 1), F32)],
        scratch_shapes=[pltpu.VMEM((L, 128), F32), pltpu.VMEM((L, DV_M), F32),
                        pltpu.VMEM((2, DK_M, DV_M), F32), pltpu.VMEM((2, 8, DK_M), F32),
                        pltpu.VMEM((2, 8, 128), F32)],
        compiler_params=pltpu.CompilerParams(
            dimension_semantics=("parallel", "arbitrary"), vmem_limit_bytes=VMEM_LIMIT_BYTES),
    )(z, z, z, z, zs, gb_row, norm_g.reshape(H_M, 1, DV_M), c0, n0, m0)
    return y, (c, n.reshape(B, 2, H_M, DK_M), m.reshape(B, 2, H_M))


def rmsnorm(x, g):
    xf = x.astype(F32)
    y = xf * lax.rsqrt(jnp.mean(xf * xf, axis=-1, keepdims=True) + EPS)
    return (y * g.astype(F32)).astype(x.dtype)


def head_rmsnorm(h, g):
    return h * lax.rsqrt(jnp.mean(h * h, axis=-1, keepdims=True) + EPS) * g.astype(F32)


def l2norm(x):
    return x * lax.rsqrt(jnp.sum(x * x, axis=-1, keepdims=True) + EPS)


def conv1d_centred(x, w):
    xp = jnp.pad(x, ((0, 0), (1, 1), (0, 0)))
    return xp[:, :-2] * w[0] + xp[:, 1:-1] * w[1] + xp[:, 2:] * w[2]


def dwconv_grid(x, w, b, rows):
    B, L, C = x.shape
    y = lax.conv_general_dilated(x.reshape(B, rows, L // rows, C), w[:, :, None, :].astype(x.dtype),
                                 window_strides=(1, 1), padding='SAME',
                                 dimension_numbers=('NHWC', 'HWIO', 'NHWC'), feature_group_count=C)
    return y.reshape(B, L, C) + b.astype(x.dtype)


def to_chunks(a):
    B, L = a.shape[:2]
    return jnp.moveaxis(a.reshape((B, L // CHUNK, CHUNK) + a.shape[2:]), 1, 0)


def from_chunks(a):
    nc, B, T = a.shape[:3]
    return jnp.moveaxis(a, 0, 1).reshape((B, nc * T) + a.shape[3:])


def flip(a):
    return jnp.flip(a, axis=1)


def hyena_filters(L, w1, b1, w2, b2, freq, w3):
    t = jnp.arange(L, dtype=F32) / L
    bands = jnp.linspace(1e-4, FILTER_BANDS - 1, FILTER_BANDS, dtype=F32)
    ang = 2.0 * math.pi * t[:, None] * bands[None, :]
    feat = jnp.concatenate([t[:, None], jnp.cos(ang), jnp.sin(ang)], axis=-1)
    fr = freq.astype(F32)
    hid = jnp.sin(fr * (feat @ w1.astype(F32) + b1.astype(F32)))
    hid = jnp.sin(fr * (hid @ w2.astype(F32) + b2.astype(F32)))
    h = (hid @ w3.astype(F32)).reshape(L, HYENA_ORDER, 2, W_H)
    deltas = jnp.abs(jnp.linspace(math.log(1e-2) / 1.5, math.log(1e-2) / 0.3, W_H, dtype=F32))
    h = h * jnp.exp(-t[:, None] * deltas[None, :])[:, None, None, :]
    return h * lax.rsqrt(jnp.sum(h * h, axis=(0, 2), keepdims=True) + EPS)


def fft_longconv(u, k_fwd, k_bwd, bias):
    L, C = k_fwd.shape
    k = jnp.concatenate([k_fwd[:1] + k_bwd[:1], k_fwd[1:], jnp.zeros((1, C), F32), k_bwd[:0:-1]], axis=0)
    y = jnp.fft.irfft(jnp.fft.rfft(u, n=2 * L, axis=1) * jnp.fft.rfft(k, axis=0)[None], n=2 * L, axis=1)[:, :L]
    return y + u * bias.astype(F32)


def hyena_mixer(zh, conv_w, filt, bias):
    zc = conv1d_centred(zh, conv_w.astype(zh.dtype)).astype(F32)
    v, x1, x2 = jnp.split(zc, 3, axis=-1)
    s = x1 * fft_longconv(v, filt[:, 0, 0], filt[:, 0, 1], bias[0])
    return x2 * fft_longconv(s, filt[:, 1, 0], filt[:, 1, 1], bias[1])


def mlstm_scan(q, k, v, log_i, log_f, C0, n0, m0):
    causal = jnp.tril(jnp.ones((CHUNK, CHUNK), dtype=bool))[None, :, :, None]

    def step(carry, inp):
        C, n, m = carry
        qc, kc, vc, ic, fc = inp
        b = jnp.cumsum(fc, axis=1)
        dmat = jnp.where(causal, b[:, :, None, :] - b[:, None, :, :] + ic[:, None, :, :], -jnp.inf)
        inter = b + m[:, None, :]
        m_t = jnp.maximum(inter, jnp.max(dmat, axis=2))
        s = jnp.einsum('bthd,bshd->btsh', qc, kc) * jnp.exp(dmat - m_t[:, :, None, :])
        w_inter = jnp.exp(inter - m_t)
        num = jnp.einsum('btsh,bshv->bthv', s, vc) + w_inter[..., None] * jnp.einsum('bthd,bhdv->bthv', qc, C)
        den = jnp.sum(s, axis=2) + w_inter * jnp.einsum('bthd,bhd->bth', qc, n)
        h = num / jnp.maximum(jnp.abs(den), jnp.exp(-m_t))[..., None]
        b_end = b[:, -1]
        g = b_end[:, None, :] - b + ic
        m_new = jnp.maximum(b_end + m, jnp.max(g, axis=1))
        wg = jnp.exp(g - m_new[:, None, :])
        carry_decay = jnp.exp(b_end + m - m_new)
        C = carry_decay[..., None, None] * C + jnp.einsum('bsh,bshd,bshv->bhdv', wg, kc, vc)
        n = carry_decay[..., None] * n + jnp.einsum('bsh,bshd->bhd', wg, kc)
        return (C, n, m_new), h

    xs = tuple(to_chunks(a) for a in (q, k, v, log_i, log_f))
    state, h = lax.scan(step, (C0, n0, m0), xs)
    return from_chunks(h), state


def mlstm_mixer(mq, mk, mv, mo, mg, gate_b, norm_g, C0, n0, m0):
    B, L, _ = mq.shape
    q = mq.astype(F32).reshape(B, L, H_M, DK_M)
    k = mk.astype(F32).reshape(B, L, H_M, DK_M) * DK_M ** -0.5
    v = mv.astype(F32).reshape(B, L, H_M, DV_M)
    gp = mg.astype(F32).reshape(B, L, 4, H_M) + gate_b.astype(F32)
    log_i = gp[:, :, 0:2]
    log_f = jax.nn.log_sigmoid(gp[:, :, 2:4])
    hf, (Cf, nf, mf) = mlstm_scan(q, k, v, log_i[:, :, 0], log_f[:, :, 0], C0[:, 0], n0[:, 0], m0[:, 0])
    hb, (Cb, nb, mb) = mlstm_scan(flip(q), flip(k), flip(v), flip(log_i[:, :, 1]), flip(log_f[:, :, 1]),
                                  C0[:, 1], n0[:, 1], m0[:, 1])
    h = head_rmsnorm(hf + flip(hb), norm_g)
    y = h.reshape(B, L, H_M * DV_M) * jax.nn.sigmoid(mo.astype(F32))
    return y, (jnp.stack([Cf, Cb], axis=1), jnp.stack([nf, nb], axis=1), jnp.stack([mf, mb], axis=1))


def gdn_scan(q, k, v, g, beta, S0):
    tri = jnp.tril(jnp.ones((CHUNK, CHUNK), dtype=bool))
    strict = jnp.tril(jnp.ones((CHUNK, CHUNK), dtype=bool), k=-1)
    eye = jnp.eye(CHUNK, dtype=F32)

    def step(S, inp):
        qc, kc, vc, gc, bc = inp
        G = jnp.swapaxes(jnp.cumsum(gc, axis=1), 1, 2)
        bh = jnp.swapaxes(bc, 1, 2)
        dec = jnp.exp(jnp.where(tri, G[..., :, None] - G[..., None, :], -jnp.inf))
        kk = jnp.einsum('bthd,bshd->bhts', kc, kc)
        a = eye + jnp.where(strict, bh[..., :, None] * kk * dec, 0.0)
        eG = jnp.exp(G)
        rhs = jnp.concatenate([(bh * eG)[..., None] * jnp.swapaxes(kc, 1, 2),
                               bh[..., None] * jnp.swapaxes(vc, 1, 2)], axis=-1)
        wu = lax.linalg.triangular_solve(a, rhs, left_side=True, lower=True, unit_diagonal=True)
        w, u = wu[..., :DK_G], wu[..., DK_G:]
        v_new = u - jnp.einsum('bhtk,bhkv->bhtv', w, S)
        qk = jnp.einsum('bthk,bshk->bhts', qc, kc) * dec
        o = eG[..., None] * jnp.einsum('bthk,bhkv->bhtv', qc, S) + jnp.einsum('bhts,bhsv->bhtv', qk, v_new)
        G_end = G[..., -1]
        S = jnp.exp(G_end)[..., None, None] * S + jnp.einsum('bhs,bshk,bhsv->bhkv',
                                                             jnp.exp(G_end[..., None] - G), kc, v_new)
        return S, jnp.swapaxes(o, 1, 2)

    xs = tuple(to_chunks(a) for a in (q, k, v, g, beta))
    S, o = lax.scan(step, S0, xs)
    return from_chunks(o), S


def gdn_mixer(gqkv, gz, gb, ga, conv_w, A_log, dt_bias, norm_g, S0):
    B, L, _ = gqkv.shape
    qkv = jax.nn.silu(conv1d_centred(gqkv, conv_w.astype(gqkv.dtype)).astype(F32))
    q, k, v = jnp.split(qkv, [H_G * DK_G, 2 * H_G * DK_G], axis=-1)
    q = l2norm(q.reshape(B, L, H_G, DK_G)) * DK_G ** -0.5
    k = l2norm(k.reshape(B, L, H_G, DK_G))
    v = v.reshape(B, L, H_G, DV_G)
    beta = jax.nn.sigmoid(gb.astype(F32).reshape(B, L, 2, H_G))
    g = -jnp.exp(A_log.astype(F32)) * jax.nn.softplus(ga.astype(F32).reshape(B, L, 2, H_G) + dt_bias.astype(F32))
    of, Sf = gdn_scan(q, k, v, g[:, :, 0], beta[:, :, 0], S0[:, 0])
    ob, Sb = gdn_scan(flip(q), flip(k), flip(v), flip(g[:, :, 1]), flip(beta[:, :, 1]), S0[:, 1])
    o = head_rmsnorm(of + flip(ob), norm_g)
    y = o.reshape(B, L, H_G * DV_G) * jax.nn.silu(gz.astype(F32))
    return y, jnp.stack([Sf, Sb], axis=1)


def mixing_block(h, p, states):
    B, L, _ = h.shape
    mC, mn, mm, gS = states
    hb = h.reshape(B * L, D_MODEL).astype(BF16)
    z2 = matmul(hb, p['w_in_main'], tm=1024, tn=512)
    zs2 = matmul(hb, p['w_in_small'], tm=1024, tn=N_SMALL_PAD)
    z = z2.reshape(B, L, -1)
    zs = zs2.reshape(B, L, -1)
    o = 0
    parts = []
    for n in (3 * W_H, H_M * DK_M, H_M * DK_M, H_M * DV_M, H_M * DV_M,
              H_G * (2 * DK_G + DV_G), H_G * DV_G, N_BRANCH * D_MODEL):
        parts.append(z[..., o:o + n])
        o += n
    zh, mq, mk, mv, mo, gqkv, gz, zm = parts
    mg, gb, ga = zs[..., 0:32], zs[..., 32:64], zs[..., 64:96]
    filt = hyena_filters(L, p['hyena_w1'], p['hyena_b1'], p['hyena_w2'], p['hyena_b2'], p['hyena_freq'], p['hyena_w3'])
    y_a = hyena_mixer(zh, p['hyena_conv_w'], filt, p['hyena_bias'])
    y_b, m_state = mlstm_pallas(z2, zs2, p['mlstm_gate_b'], p['mlstm_norm'],
                                None if mC is None else (mC, mn, mm), B, L)
    y_c, g_state = gdn_pallas(z2, zs2, p['gdn_conv_w'], p['gdn_A_log'], p['gdn_dt_bias'], p['gdn_norm'], gS, B, L)
    gates = jax.nn.sigmoid(zm).reshape(B, L, N_BRANCH, D_MODEL)
    wb = p['w_branch']
    mix = 0.0
    for i, y in enumerate((y_a, y_b, y_c)):
        pr = matmul(y.reshape(B * L, BRANCH_W).astype(BF16), wb[i], tm=1024, tn=512).reshape(B, L, D_MODEL)
        mix = mix + gates[:, :, i] * pr
    out = matmul(mix.reshape(B * L, D_MODEL).astype(BF16), p['w_out'], tm=1024, tn=512)
    return out.reshape(B, L, D_MODEL), m_state + (g_state,)


def trunk_layer(x, mod, p, rows, states):
    B, L, _ = x.shape
    sh1, sc1, g1, sh2, sc2, g2 = mod
    h = rmsnorm(x, p['norm_mix']) * (1.0 + sc1) + sh1
    mix, new_states = mixing_block(h, p, states)
    x = x + g1 * mix
    h = rmsnorm(x, p['norm_ffn']) * (1.0 + sc2) + sh2
    hb = h.reshape(B * L, D_MODEL).astype(BF16)
    gate_pre = matmul(hb, p['ffn_w_gate'], tm=1024, tn=1024).reshape(B, L, D_FF_PAD)
    up = matmul(hb, p['ffn_w_up'], tm=1024, tn=1024).reshape(B, L, D_FF_PAD)
    gate = dwconv_grid(gate_pre, p['ffn_conv_w'], p['ffn_conv_b'], rows)
    act = (jax.nn.silu(gate) * up).reshape(B * L, D_FF_PAD).astype(BF16)
    down = matmul(act, p['ffn_w_down'], tm=1024, tn=512, tk=D_FF_PAD // 2).reshape(B, L, D_MODEL)
    x = x + g2 * down
    return x, new_states


def _ada_all(c, c_ctx, ada_w, ada_b):
    cc = jnp.concatenate([c, c_ctx[None, :], jnp.zeros((7, D_MODEL), F32)], axis=0)
    a = jax.nn.silu(cc)
    outs = []
    for l in range(DEPTH):
        outs.append(matmul(a, ada_w[l], tm=16, tn=1024) + ada_b[l])
    return outs


def kernel(x_prompt, x_sample, state_mlstm_C, state_mlstm_n, state_mlstm_m, state_gdn_S, c, c_ctx,
           ada_w, ada_b, norm_mix, norm_ffn, w_in, hyena_conv_w, hyena_w1, hyena_b1, hyena_w2, hyena_b2,
           hyena_freq, hyena_w3, hyena_bias, mlstm_gate_b, mlstm_norm, gdn_conv_w, gdn_A_log, gdn_dt_bias,
           gdn_norm, w_branch, w_out, ffn_w_gate, ffn_w_up, ffn_conv_w, ffn_conv_b, ffn_w_down, final_norm):
    bp = x_prompt.shape[0]
    nb = x_sample.shape[0]
    rows_lat = x_sample.shape[1] // GRID_W
    zero_states = (None, None, None, None)
    ada = _ada_all(c, c_ctx, ada_w, ada_b)
    ffpad = D_FF_PAD - D_FF
    xp, xs = x_prompt, x_sample
    out_C, out_n, out_m, out_S = [], [], [], []
    for l in range(DEPTH):
        wl = w_in[l]
        w_main = jnp.concatenate([wl[:, :_OFF[5]], wl[:, _OFF[6]:_OFF[8]], wl[:, _OFF[10]:]], axis=1).astype(BF16)
        w_small = jnp.concatenate([wl[:, _OFF[5]:_OFF[6]], wl[:, _OFF[8]:_OFF[10]],
                                   jnp.zeros((D_MODEL, N_SMALL_PAD - 96), F32)], axis=1).astype(BF16)
        p = {'norm_mix': norm_mix[l], 'norm_ffn': norm_ffn[l], 'w_in_main': w_main, 'w_in_small': w_small,
             'hyena_conv_w': hyena_conv_w[l], 'hyena_w1': hyena_w1[l], 'hyena_b1': hyena_b1[l],
             'hyena_w2': hyena_w2[l], 'hyena_b2': hyena_b2[l], 'hyena_freq': hyena_freq[l],
             'hyena_w3': hyena_w3[l], 'hyena_bias': hyena_bias[l],
             'mlstm_gate_b': mlstm_gate_b[l], 'mlstm_norm': mlstm_norm[l],
             'gdn_conv_w': gdn_conv_w[l], 'gdn_A_log': gdn_A_log[l], 'gdn_dt_bias': gdn_dt_bias[l],
             'gdn_norm': gdn_norm[l], 'w_branch': w_branch[l].astype(BF16), 'w_out': w_out[l].astype(BF16),
             'ffn_w_gate': jnp.pad(ffn_w_gate[l], ((0, 0), (0, ffpad))).astype(BF16),
             'ffn_w_up': jnp.pad(ffn_w_up[l], ((0, 0), (0, ffpad))).astype(BF16),
             'ffn_conv_w': jnp.pad(ffn_conv_w[l], ((0, 0), (0, 0), (0, ffpad))),
             'ffn_conv_b': jnp.pad(ffn_conv_b[l], ((0, ffpad),)),
             'ffn_w_down': jnp.pad(ffn_w_down[l], ((0, ffpad), (0, 0))).astype(BF16)}
        m = ada[l]
        mod_ctx = [t[None, None, :] for t in jnp.split(m[nb], 6, axis=-1)]
        mod_lat = [t[:, None, :] for t in jnp.split(m[:nb], 6, axis=-1)]
        xp, (sC, sn, sm, sS) = trunk_layer(xp, mod_ctx, p, 1, zero_states)
        out_C.append(sC)
        out_n.append(sn)
        out_m.append(sm)
        out_S.append(sS)
        cached = (state_mlstm_C[:, l].astype(F32), state_mlstm_n[:, l].astype(F32),
                  state_mlstm_m[:, l].astype(F32), state_gdn_S[:, l].astype(F32))
        xs, _ = trunk_layer(xs, mod_lat, p, rows_lat, cached)
    y_prompt = rmsnorm(xp, final_norm)
    y_sample = rmsnorm(xs, final_norm)
    return (y_prompt, y_sample, jnp.stack(out_C, axis=1), jnp.stack(out_n, axis=1),
            jnp.stack(out_m, axis=1), jnp.stack(out_S, axis=1))
```

```python
import functools
import math

import jax
import jax.numpy as jnp
import numpy as np
from jax import lax
from jax.experimental import pallas as pl
from jax.experimental.pallas import tpu as pltpu

D_MODEL = 4096
DEPTH = 2
GRID_W = 64
N_BRANCH = 3
BRANCH_W = D_MODEL // 2
W_H = BRANCH_W
HYENA_ORDER = 2
FILTER_BANDS = 16
H_M = 8
DV_M = BRANCH_W // H_M
DK_M = DV_M // 2
H_G = 16
DK_G = BRANCH_W // H_G
DV_G = BRANCH_W // H_G
D_FF = 11008
D_FF_PAD = 11264
CHUNK = 64
EPS = 1e-6
F32 = jnp.float32
BF16 = jnp.bfloat16

_IN_SPLITS = (3 * W_H, H_M * DK_M, H_M * DK_M, H_M * DV_M, H_M * DV_M, 4 * H_M,
              H_G * (2 * DK_G + DV_G), H_G * DV_G, 2 * H_G, 2 * H_G, N_BRANCH * D_MODEL)
_OFF = np.concatenate([[0], np.cumsum(_IN_SPLITS)]).tolist()
N_SMALL_PAD = 128

VMEM_LIMIT_BYTES = 56 * 1024 * 1024


def _mm_kernel(x_ref, w_ref, o_ref, *, nk):
    part = jnp.dot(x_ref[...].astype(BF16), w_ref[...].astype(BF16), preferred_element_type=F32)
    if nk == 1:
        o_ref[...] = part
    else:
        k = pl.program_id(2)

        @pl.when(k == 0)
        def _():
            o_ref[...] = part

        @pl.when(k != 0)
        def _():
            o_ref[...] += part


def matmul(x, w, *, tm, tn, tk=None):
    M, K = x.shape
    K2, N = w.shape
    assert K == K2
    tk = K if tk is None else tk
    tm = min(tm, M)
    assert M % tm == 0 and N % tn == 0 and K % tk == 0, (M, N, K, tm, tn, tk)
    nk = K // tk
    return pl.pallas_call(
        functools.partial(_mm_kernel, nk=nk),
        grid=(M // tm, N // tn, nk),
        in_specs=[pl.BlockSpec((tm, tk), lambda i, j, k: (i, k)),
                  pl.BlockSpec((tk, tn), lambda i, j, k: (k, j))],
        out_specs=pl.BlockSpec((tm, tn), lambda i, j, k: (i, j)),
        out_shape=jax.ShapeDtypeStruct((M, N), F32),
        compiler_params=pltpu.CompilerParams(
            dimension_semantics=("parallel", "parallel", "arbitrary"),
            vmem_limit_bytes=VMEM_LIMIT_BYTES),
    )(x, w)


def _dot(a, b):
    return jnp.dot(a.astype(BF16), b.astype(BF16), preferred_element_type=F32)


def _dot_nt(a, b):
    return lax.dot_general(a.astype(BF16), b.astype(BF16), (((1,), (1,)), ((), ())), preferred_element_type=F32)


def _dot_tn(a, b):
    return lax.dot_general(a.astype(BF16), b.astype(BF16), (((0,), (0,)), ((), ())), preferred_element_type=F32)


def _split2(a):
    hi = a.astype(BF16)
    return hi, (a - hi.astype(F32)).astype(BF16)


def _split3(a):
    hi = a.astype(BF16)
    r = a - hi.astype(F32)
    mid = r.astype(BF16)
    return hi, mid, (r - mid.astype(F32)).astype(BF16)


def _dot3(a, b):
    ah, al = _split2(a)
    bh, bl = _split2(b)
    return (jnp.dot(ah, bl, preferred_element_type=F32) + jnp.dot(al, bh, preferred_element_type=F32)
            + jnp.dot(ah, bh, preferred_element_type=F32))


def _dot_exact_lhs(a_bf16, b):
    b0, b1, b2 = _split3(b)
    return (jnp.dot(a_bf16, b2, preferred_element_type=F32) + jnp.dot(a_bf16, b1, preferred_element_type=F32)
            + jnp.dot(a_bf16, b0, preferred_element_type=F32))


def _inv_unit_triangular(ns, eye):
    xs = [eye - n for n in ns]
    ps = list(ns)
    for _ in range(5):
        ps = [_dot(p, p) for p in ps]
        xs = [x + _dot(x, p) for x, p in zip(xs, ps)]
    rs = [eye - x - _dot3(n, x) for n, x in zip(ns, xs)]
    return [x + _dot(x, r) for x, r in zip(xs, rs)]


def _softplus(x):
    return jnp.maximum(x, 0.0) + jnp.log1p(jnp.exp(-jnp.abs(x)))


def _lane_pick(x, lane):
    idx = lax.broadcasted_iota(jnp.int32, x.shape, 1)
    return jnp.sum(jnp.where(idx == lane, x, 0.0), axis=-1, keepdims=True)


def _conv3_rows(ref, w_ref, r0, nrows, total):
    x = ref[pl.ds(r0, nrows), :]
    row = lax.broadcasted_iota(jnp.int32, x.shape, 0)
    p0 = jnp.maximum(r0 - 8, 0)
    n0 = jnp.minimum(r0 + nrows, total - 8)
    prev = ref[pl.ds(pl.multiple_of(p0, 8), 8), :][7:8, :] * (r0 > 0).astype(F32)
    nxt = ref[pl.ds(pl.multiple_of(n0, 8), 8), :][0:1, :] * (r0 + nrows < total).astype(F32)
    xm1 = jnp.where(row == 0, prev, pltpu.roll(x, 1, 0))
    xp1 = jnp.where(row == nrows - 1, nxt, pltpu.roll(x, nrows - 1, 0))
    return xm1 * w_ref[0:1, :] + x * w_ref[1:2, :] + xp1 * w_ref[2:3, :]


GDN_LANE_BETA = 32
GDN_LANE_DECAY = 64
PREP_ROWS = 256
GDN_GROUP = 4


def _gdn_kernel(zq_ref, zk_ref, zv_ref, gz_ref, zs_ref, cwq_ref, cwk_ref, cwv_ref, alog_ref, dtb_ref, ng_ref,
                s0_ref, y_ref, sout_ref,
                q_s, k_s, v_s, gall_s, ball_s, o_s, wq_s, u_s, qkd_s, kd_s, eend_s, st_s, *, L, zero_init):
    h = pl.program_id(1)
    nc = L // CHUNK
    T = CHUNK
    row = lax.broadcasted_iota(jnp.int32, (T, T), 0)
    col = lax.broadcasted_iota(jnp.int32, (T, T), 1)
    eye = (row == col).astype(F32)
    ones_b = jnp.ones((T, T), BF16)
    incl = (row >= col, row <= col)
    strict = (row > col, row < col)

    @pl.when(h == 0)
    def _():
        lane = lax.broadcasted_iota(jnp.int32, (T, 128), 1)
        tril_b = incl[0].astype(BF16)
        triu_b = incl[1].astype(BF16)

        def seg(c, carry):
            r = pl.ds(pl.multiple_of(c * T, T), T)
            zs = zs_ref[r, :]
            ball_s[r, :] = jax.nn.sigmoid(zs)
            g = -jnp.exp(alog_ref[...]) * _softplus(zs + dtb_ref[...])
            pre = _dot_exact_lhs(tril_b, g)
            suf = _dot_exact_lhs(triu_b, g)
            gall_s[r, :] = jnp.where(lane < GDN_LANE_DECAY + H_G, pre, suf)
            return carry

        lax.fori_loop(0, nc, seg, 0)

    def prep(i, carry):
        r0 = pl.multiple_of(i * PREP_ROWS, PREP_ROWS)
        r = pl.ds(r0, PREP_ROWS)
        q = _conv3_rows(zq_ref, cwq_ref, r0, PREP_ROWS, L)
        q = q * jax.nn.sigmoid(q)
        q = q * lax.rsqrt(jnp.sum(q * q, axis=-1, keepdims=True) + EPS) * DK_G ** -0.5
        q_s[r, :] = q.astype(BF16)
        o_s[r, :] = jnp.zeros((PREP_ROWS, 128), F32)
        k = _conv3_rows(zk_ref, cwk_ref, r0, PREP_ROWS, L)
        k = k * jax.nn.sigmoid(k)
        k_s[r, :] = k * lax.rsqrt(jnp.sum(k * k, axis=-1, keepdims=True) + EPS)
        v = _conv3_rows(zv_ref, cwv_ref, r0, PREP_ROWS, L)
        v_s[r, :] = v * jax.nn.sigmoid(v)
        return carry

    lax.fori_loop(0, L // PREP_ROWS, prep, 0)

    def phase_a(gi, carry):
        cs = [gi * GDN_GROUP + j for j in range(GDN_GROUP)]
        rs = [pl.ds(pl.multiple_of(c * T, T), T) for c in cs]
        qs = [q_s[r, :] for r in rs]
        ks = [k_s[r, :] for r in rs]
        vs = [v_s[r, :] for r in rs]
        kks = [_dot_nt(k, k) for k in ks]
        qks = [_dot_nt(q, k) for q, k in zip(qs, ks)]
        items = [(j, d) for j in range(GDN_GROUP) for d in range(2)]
        gcs = [_lane_pick(gall_s[rs[j], :], GDN_LANE_DECAY + d * H_G + h) for j, d in items]
        bcs = [_lane_pick(ball_s[rs[j], :], GDN_LANE_BETA + d * H_G + h) for j, d in items]
        gsq = [jnp.broadcast_to(g, (T, T)) for g in gcs]
        grs = [_dot_exact_lhs(ones_b, jnp.where(row == col, g, 0.0)) for g in gsq]
        decs = [jnp.exp(jnp.where(incl[d], g - gr, -1e30)) for (j, d), g, gr in zip(items, gsq, grs)]
        ns = [jnp.where(strict[d], b * kks[j] * dec, 0.0) for (j, d), b, dec in zip(items, bcs, decs)]
        tinvs = _inv_unit_triangular(ns, eye)
        egs = [jnp.exp(g) for g in gcs]
        ws = [_dot3(t, (b * eg) * ks[j]) for (j, d), t, b, eg in zip(items, tinvs, bcs, egs)]
        us = [_dot3(t, b * vs[j]) for (j, d), t, b in zip(items, tinvs, bcs)]
        for idx, (j, d) in enumerate(items):
            c, r, gc = cs[j], rs[j], gcs[idx]
            g_end = gc[T - 1:T, :] if d == 0 else gc[0:1, :]
            wq_s[d, pl.ds(pl.multiple_of(c * 2 * T, 2 * T), T), :] = ws[idx].astype(BF16)
            wq_s[d, pl.ds(pl.multiple_of(c * 2 * T, 2 * T) + T, T), :] = (egs[idx] * qs[j]).astype(BF16)
            u_s[d, r, :] = us[idx]
            qkd_s[d, r, :] = (qks[j] * decs[idx]).astype(BF16)
            kd_s[d, r, :] = (jnp.exp(g_end - gc) * ks[j]).astype(BF16)
            eend_s[d, pl.ds(pl.multiple_of(c * 8, 8), 8), :] = jnp.broadcast_to(jnp.exp(g_end), (8, 128))
        return carry

    lax.fori_loop(0, nc // GDN_GROUP, phase_a, 0)

    for d in range(2):
        if zero_init:
            st_s[d] = jnp.zeros((DK_G, DV_G), F32)
        else:
            st_s[d] = s0_ref[0, d, 0]

    def phase_b(i, carry):
        cs = (i, nc - 1 - i)
        rs = [pl.ds(pl.multiple_of(c * T, T), T) for c in cs]
        ss = [st_s[d] for d in range(2)]
        wss = [jnp.dot(wq_s[d, pl.ds(pl.multiple_of(cs[d] * 2 * T, 2 * T), 2 * T), :], ss[d].astype(BF16),
                       preferred_element_type=F32) for d in range(2)]
        vns = [(u_s[d, rs[d], :] - wss[d][:T]).astype(BF16) for d in range(2)]
        os_ = [jnp.dot(qkd_s[d, rs[d], :], vns[d], preferred_element_type=F32) for d in range(2)]
        kvs = [_dot_tn(kd_s[d, rs[d], :], vns[d]) for d in range(2)]
        for d in range(2):
            e_end = eend_s[d, pl.ds(pl.multiple_of(cs[d] * 8, 8), 8), :][0:1, :]
            st_s[d] = e_end * ss[d] + kvs[d]
        for d in range(2):
            o_s[rs[d], :] += wss[d][T:] + os_[d]
        return carry

    lax.fori_loop(0, nc, phase_b, 0)
    for d in range(2):
        sout_ref[0, d, 0] = st_s[d]

    def fin(i, carry):
        r = pl.ds(pl.multiple_of(i * PREP_ROWS, PREP_ROWS), PREP_ROWS)
        o = o_s[r, :]
        o = o * lax.rsqrt(jnp.mean(o * o, axis=-1, keepdims=True) + EPS) * ng_ref[...]
        z = gz_ref[r, :]
        y_ref[r, :] = (o * (z * jax.nn.sigmoid(z))).astype(y_ref.dtype)
        return carry

    lax.fori_loop(0, L // PREP_ROWS, fin, 0)


Z_COL_GQ = (3 * W_H + 2 * H_M * DK_M + 2 * H_M * DV_M) // 128
Z_COL_GZ = Z_COL_GQ + 3 * H_G


def gdn_pallas(z, zs, conv_w, a_log, dt_bias, norm_g, s0, B, L):
    zero_init = s0 is None
    if zero_init:
        s0 = jnp.zeros((1, 2, 1, DK_G, DV_G), F32)
        s0_map = lambda b, h: (0, 0, 0, 0, 0)
    else:
        s0_map = lambda b, h: (b, 0, h, 0, 0)
    lanes = jnp.zeros((1, 128), F32)
    alog_row = lanes.at[0, GDN_LANE_DECAY:GDN_LANE_DECAY + 2 * H_G].set(a_log.reshape(-1))
    dtb_row = lanes.at[0, GDN_LANE_DECAY:GDN_LANE_DECAY + 2 * H_G].set(dt_bias.reshape(-1))
    nc = L // CHUNK
    col = lambda off: pl.BlockSpec((L, 128), lambda b, h: (b, off + h))
    cw = lambda off: pl.BlockSpec((3, 128), lambda b, h: (0, off + h))
    one = pl.BlockSpec((1, 128), lambda b, h: (0, 0))
    st = lambda m: pl.BlockSpec((1, 2, 1, DK_G, DV_G), m)
    return pl.pallas_call(
        functools.partial(_gdn_kernel, L=L, zero_init=zero_init),
        grid=(B, H_G),
        in_specs=[col(Z_COL_GQ), col(Z_COL_GQ + H_G), col(Z_COL_GQ + 2 * H_G), col(Z_COL_GZ),
                  pl.BlockSpec((L, 128), lambda b, h: (b, 0)),
                  cw(0), cw(H_G), cw(2 * H_G), one, one, one, st(s0_map)],
        out_specs=[pl.BlockSpec((L, 128), lambda b, h: (b, h)), st(lambda b, h: (b, 0, h, 0, 0))],
        out_shape=[jax.ShapeDtypeStruct((B * L, H_G * DV_G), BF16),
                   jax.ShapeDtypeStruct((B, 2, H_G, DK_G, DV_G), F32)],
        scratch_shapes=[pltpu.VMEM((L, 128), BF16), pltpu.VMEM((L, 128), F32), pltpu.VMEM((L, 128), F32),
                        pltpu.VMEM((L, 128), F32), pltpu.VMEM((L, 128), F32), pltpu.VMEM((L, 128), F32),
                        pltpu.VMEM((2, 2 * L, 128), BF16), pltpu.VMEM((2, L, 128), F32),
                        pltpu.VMEM((2, L, CHUNK), BF16), pltpu.VMEM((2, L, 128), BF16),
                        pltpu.VMEM((2, nc * 8, 128), F32), pltpu.VMEM((2, DK_G, DV_G), F32)],
        compiler_params=pltpu.CompilerParams(
            dimension_semantics=("parallel", "arbitrary"), vmem_limit_bytes=VMEM_LIMIT_BYTES),
    )(z, z, z, z, zs, conv_w, conv_w, conv_w, alog_row, dtb_row, norm_g.reshape(1, DV_G), s0)


MLSTM_LANE_I = 0
MLSTM_LANE_F = 2 * H_M
NEG_BIG = -1e30


def _mlstm_kernel(q_ref, k_ref, v_ref, mo_ref, zs_ref, gb_ref, ng_ref, c0_ref, n0_ref, m0_ref,
                  y_ref, cout_ref, nout_ref, mout_ref, gall_s, h_s, c_s, n_s, m_s, *, L, zero_init):
    h = pl.program_id(1)
    nc = L // CHUNK
    T = CHUNK
    row = lax.broadcasted_iota(jnp.int32, (T, T), 0)
    col = lax.broadcasted_iota(jnp.int32, (T, T), 1)
    ones_b = jnp.ones((T, T), BF16)
    incl = (row >= col, row <= col)

    @pl.when(h == 0)
    def _():
        lane = lax.broadcasted_iota(jnp.int32, (T, 128), 1)
        tril_b = incl[0].astype(BF16)
        triu_b = incl[1].astype(BF16)

        def seg(c, carry):
            r = pl.ds(pl.multiple_of(c * T, T), T)
            gp = zs_ref[r, :] + gb_ref[...]
            lf = jnp.minimum(gp, 0.0) - jnp.log1p(jnp.exp(-jnp.abs(gp)))
            pre = _dot_exact_lhs(tril_b, lf)
            suf = _dot_exact_lhs(triu_b, lf)
            gall_s[r, :] = jnp.where(lane < MLSTM_LANE_F, gp, jnp.where(lane < MLSTM_LANE_F + H_M, pre, suf))
            return carry

        lax.fori_loop(0, nc, seg, 0)

    for d in range(2):
        if zero_init:
            c_s[d] = jnp.zeros((DK_M, DV_M), F32)
            n_s[d] = jnp.zeros((8, DK_M), F32)
            m_s[d] = jnp.zeros((8, 128), F32)
        else:
            c_s[d] = c0_ref[0, d, 0]
            n_s[d] = jnp.broadcast_to(n0_ref[0, d, 0], (8, DK_M))
            m_s[d] = jnp.broadcast_to(m0_ref[0, d, 0], (8, 128))

    def zero_h(i, carry):
        h_s[pl.ds(pl.multiple_of(i * PREP_ROWS, PREP_ROWS), PREP_ROWS), :] = jnp.zeros((PREP_ROWS, DV_M), F32)
        return carry

    lax.fori_loop(0, L // PREP_ROWS, zero_h, 0)

    def step(i, carry):
        D = range(2)
        cs = (i, nc - 1 - i)
        rs = [pl.ds(pl.multiple_of(c * T, T), T) for c in cs]
        qs = [q_ref[r, :] for r in rs]
        ks = [k_ref[r, :] * DK_M ** -0.5 for r in rs]
        vs = [v_ref[r, :].astype(BF16) for r in rs]
        qbs = [q.astype(BF16) for q in qs]
        qks = [_dot_nt(qbs[d], ks[d]) for d in D]
        cst = [c_s[d] for d in D]
        qcs = [jnp.dot(qbs[d], cst[d].astype(BF16), preferred_element_type=F32) for d in D]
        bcol = [_lane_pick(gall_s[rs[d], :], MLSTM_LANE_F + d * H_M + h) for d in D]
        icol = [_lane_pick(gall_s[rs[d], :], MLSTM_LANE_I + d * H_M + h) for d in D]
        amb = [icol[d] - bcol[d] for d in D]
        rmat = [_dot_exact_lhs(ones_b, jnp.where(row == col, jnp.broadcast_to(amb[d], (T, T)), 0.0)) for d in D]
        dmat = [jnp.where(incl[d], bcol[d] + rmat[d], NEG_BIG) for d in D]
        m = [m_s[d][0:1, 0:1] for d in D]
        inter = [bcol[d] + m[d] for d in D]
        m_t = [jnp.maximum(inter[d], jnp.max(dmat[d], axis=-1, keepdims=True)) for d in D]
        smat = [qks[d] * jnp.exp(dmat[d] - m_t[d]) for d in D]
        w_inter = [jnp.exp(inter[d] - m_t[d]) for d in D]
        sv = [jnp.dot(smat[d].astype(BF16), vs[d], preferred_element_type=F32) for d in D]
        b_end = [bcol[0][T - 1:T, :], bcol[1][0:1, :]]
        gcol = [b_end[d] + amb[d] for d in D]
        m_new = [jnp.maximum(b_end[d] + m[d], jnp.max(gcol[d], axis=0, keepdims=True)) for d in D]
        kw = [jnp.exp(gcol[d] - m_new[d]) * ks[d] for d in D]
        kv = [_dot_tn(kw[d], vs[d]) for d in D]
        for d in D:
            n = n_s[d][0:1, :]
            cd = jnp.exp(b_end[d] + m[d] - m_new[d])
            num = sv[d] + w_inter[d] * qcs[d]
            den = (jnp.sum(smat[d], axis=-1, keepdims=True)
                   + w_inter[d] * jnp.sum(qs[d] * n, axis=-1, keepdims=True))
            h_s[rs[d], :] += num / jnp.maximum(jnp.abs(den), jnp.exp(-m_t[d]))
            c_s[d] = cd * cst[d] + kv[d]
            n_s[d] = jnp.broadcast_to(cd * n + jnp.sum(kw[d], axis=0, keepdims=True), (8, DK_M))
            m_s[d] = jnp.broadcast_to(m_new[d], (8, 128))
        return carry

    lax.fori_loop(0, nc, step, 0)
    for d in range(2):
        cout_ref[0, d, 0] = c_s[d]
        nout_ref[0, d, 0] = n_s[d][0:1, :]
        mout_ref[0, d, 0] = m_s[d][0:1, 0:1]

    def fin(i, carry):
        r = pl.ds(pl.multiple_of(i * PREP_ROWS, PREP_ROWS), PREP_ROWS)
        o = h_s[r, :]
        o = o * lax.rsqrt(jnp.mean(o * o, axis=-1, keepdims=True) + EPS) * ng_ref[0]
        y_ref[r, :] = (o * jax.nn.sigmoid(mo_ref[r, :])).astype(y_ref.dtype)
        return carry

    lax.fori_loop(0, L // PREP_ROWS, fin, 0)


Z_COL_MQ = 3 * W_H // 128
Z_COL_MV = (3 * W_H + 2 * H_M * DK_M) // DV_M


def mlstm_pallas(z, zs, gate_b, norm_g, state0, B, L):
    zero_init = state0 is None
    if zero_init:
        c0 = jnp.zeros((1, 2, 1, DK_M, DV_M), F32)
        n0 = jnp.zeros((1, 2, 1, 1, DK_M), F32)
        m0 = jnp.zeros((1, 2, 1, 1, 1), F32)
        smap = lambda b, h: (0, 0, 0, 0, 0)
    else:
        c0 = state0[0]
        n0 = state0[1].reshape(B, 2, H_M, 1, DK_M)
        m0 = state0[2].reshape(B, 2, H_M, 1, 1)
        smap = lambda b, h: (b, 0, h, 0, 0)
    omap = lambda b, h: (b, 0, h, 0, 0)
    gb_row = jnp.zeros((1, 128), F32).at[0, :4 * H_M].set(gate_b.reshape(-1))
    st_specs = lambda m: [pl.BlockSpec((1, 2, 1, DK_M, DV_M), m), pl.BlockSpec((1, 2, 1, 1, DK_M), m),
                          pl.BlockSpec((1, 2, 1, 1, 1), m)]
    y, c, n, m = pl.pallas_call(
        functools.partial(_mlstm_kernel, L=L, zero_init=zero_init),
        grid=(B, H_M),
        in_specs=[pl.BlockSpec((L, DK_M), lambda b, h: (b, Z_COL_MQ + h)),
                  pl.BlockSpec((L, DK_M), lambda b, h: (b, Z_COL_MQ + H_M + h)),
                  pl.BlockSpec((L, DV_M), lambda b, h: (b, Z_COL_MV + h)),
                  pl.BlockSpec((L, DV_M), lambda b, h: (b, Z_COL_MV + H_M + h)),
                  pl.BlockSpec((L, 128), lambda b, h: (b, 0)),
                  pl.BlockSpec((1, 128), lambda b, h: (0, 0)),
                  pl.BlockSpec((1, 1, DV_M), lambda b, h: (h, 0, 0))] + st_specs(smap),
        out_specs=[pl.BlockSpec((L, DV_M), lambda b, h: (b, h))] + st_specs(omap),
        out_shape=[jax.ShapeDtypeStruct((B * L, H_M * DV_M), BF16),
                   jax.ShapeDtypeStruct((B, 2, H_M, DK_M, DV_M), F32),
                   jax.ShapeDtypeStruct((B, 2, H_M, 1, DK_M), F32),
                   jax.ShapeDtypeStruct((B, 2, H_M, 1, 1), F32)],
        scratch_shapes=[pltpu.VMEM((L, 128), F32), pltpu.VMEM((L, DV_M), F32),
                        pltpu.VMEM((2, DK_M, DV_M), F32), pltpu.VMEM((2, 8, DK_M), F32),
                        pltpu.VMEM((2, 8, 128), F32)],
        compiler_params=pltpu.CompilerParams(
            dimension_semantics=("parallel", "arbitrary"), vmem_limit_bytes=VMEM_LIMIT_BYTES),
    )(z, z, z, z, zs, gb_row, norm_g.reshape(H_M, 1, DV_M), c0, n0, m0)
    return y, (c, n.reshape(B, 2, H_M, DK_M), m.reshape(B, 2, H_M))


FBLK = 256
HY_TN = 512
HY_CONV_TN = 128


def dft_matrices(L):
    n = 2 * L
    nf = L + 1
    nfb = -(-nf // FBLK)
    nfp = nfb * FBLK
    f = jnp.arange(nfp, dtype=jnp.int32)
    t = jnp.arange(L, dtype=jnp.int32)
    ang = ((f[:, None] * t[None, :]) % n).astype(F32) * (2.0 * math.pi / n)
    valid = (f < nf)[:, None]
    c = jnp.where(valid, jnp.cos(ang), 0.0)
    s = jnp.where(valid, jnp.sin(ang), 0.0)
    fwd = jnp.concatenate([c.reshape(nfb, FBLK, L), -s.reshape(nfb, FBLK, L)], axis=1).reshape(2 * nfp, L)
    wgt = jnp.where((f == 0) | (f == L), 1.0, 2.0)[:, None] / n
    inv = jnp.concatenate([(wgt * c).reshape(nfb, FBLK, L), (-wgt * s).reshape(nfb, FBLK, L)], axis=1)
    return fwd, inv.reshape(2 * nfp, L).T


def _hy_conv_kernel(v_ref, x1_ref, x2_ref, wv_ref, w1_ref, w2_ref, v32_ref, vb_ref, x1o_ref, x2o_ref, *, L):
    def body(i, carry):
        r0 = pl.multiple_of(i * PREP_ROWS, PREP_ROWS)
        r = pl.ds(r0, PREP_ROWS)
        v = _conv3_rows(v_ref, wv_ref, r0, PREP_ROWS, L)
        v32_ref[r, :] = v
        vb_ref[r, :] = v.astype(BF16)
        x1o_ref[r, :] = _conv3_rows(x1_ref, w1_ref, r0, PREP_ROWS, L)
        x2o_ref[r, :] = _conv3_rows(x2_ref, w2_ref, r0, PREP_ROWS, L)
        return carry

    lax.fori_loop(0, L // PREP_ROWS, body, 0)


def hyena_short_conv(z, conv_w, B, L):
    nj = W_H // HY_CONV_TN
    col = lambda off: pl.BlockSpec((L, HY_CONV_TN), lambda b, j: (b, off * nj + j))
    cw = lambda off: pl.BlockSpec((3, HY_CONV_TN), lambda b, j: (0, off * nj + j))
    out = pl.BlockSpec((L, HY_CONV_TN), lambda b, j: (b, j))
    f32o = jax.ShapeDtypeStruct((B * L, W_H), F32)
    return pl.pallas_call(
        functools.partial(_hy_conv_kernel, L=L),
        grid=(B, nj),
        in_specs=[col(0), col(1), col(2), cw(0), cw(1), cw(2)],
        out_specs=[out, out, out, out],
        out_shape=[f32o, jax.ShapeDtypeStruct((B * L, W_H), BF16), f32o, f32o],
        compiler_params=pltpu.CompilerParams(
            dimension_semantics=("parallel", "parallel"), vmem_limit_bytes=VMEM_LIMIT_BYTES),
    )(z, z, z, conv_w, conv_w, conv_w)


def _dft_fwd_kernel(f_ref, u_ref, k_ref, o_ref):
    acc = jnp.dot(f_ref[...], u_ref[...], preferred_element_type=F32)
    re, im = acc[:FBLK], acc[FBLK:]
    kre, kim = k_ref[:FBLK, :], k_ref[FBLK:, :]
    o_ref[:FBLK, :] = (re * kre - im * kim).astype(o_ref.dtype)
    o_ref[FBLK:, :] = (re * kim + im * kre).astype(o_ref.dtype)


def dft_filter_fwd(fwd_b, ub, kspec, B, L):
    m2 = fwd_b.shape[0]
    nfb = m2 // (2 * FBLK)
    C = ub.shape[1]
    return pl.pallas_call(
        _dft_fwd_kernel,
        grid=(B, C // HY_TN, nfb),
        in_specs=[pl.BlockSpec((2 * FBLK, L), lambda b, j, i: (i, 0)),
                  pl.BlockSpec((L, HY_TN), lambda b, j, i: (b, j)),
                  pl.BlockSpec((2 * FBLK, HY_TN), lambda b, j, i: (i, j))],
        out_specs=pl.BlockSpec((2 * FBLK, HY_TN), lambda b, j, i: (b * nfb + i, j)),
        out_shape=jax.ShapeDtypeStruct((B * m2, C), BF16),
        compiler_params=pltpu.CompilerParams(
            dimension_semantics=("parallel", "parallel", "arbitrary"), vmem_limit_bytes=VMEM_LIMIT_BYTES),
    )(fwd_b, ub, kspec)


def _dft_inv_kernel(g_ref, s_ref, xg_ref, u_ref, b_ref, *o_refs):
    y = jnp.dot(g_ref[...], s_ref[...], preferred_element_type=F32)
    u = u_ref[...]
    out = xg_ref[...] * (y + u * b_ref[...])
    for o_ref in o_refs:
        o_ref[...] = out.astype(o_ref.dtype)


def dft_inv_gate(inv_b, spec, xg, u, bias, B, L, out_dtypes):
    m2 = inv_b.shape[1]
    C = spec.shape[1]
    tm = min(512, L)
    blk = pl.BlockSpec((tm, HY_TN), lambda b, j, i: (b * (L // tm) + i, j))
    return pl.pallas_call(
        _dft_inv_kernel,
        grid=(B, C // HY_TN, L // tm),
        in_specs=[pl.BlockSpec((tm, m2), lambda b, j, i: (i, 0)),
                  pl.BlockSpec((m2, HY_TN), lambda b, j, i: (b, j)),
                  blk, blk, pl.BlockSpec((1, HY_TN), lambda b, j, i: (0, j))],
        out_specs=[blk for _ in out_dtypes],
        out_shape=[jax.ShapeDtypeStruct((B * L, C), dt) for dt in out_dtypes],
        compiler_params=pltpu.CompilerParams(
            dimension_semantics=("parallel", "parallel", "arbitrary"), vmem_limit_bytes=VMEM_LIMIT_BYTES),
    )(inv_b, spec, xg, u, bias)


def hyena_filter_spectra(filt, fwd):
    L = filt.shape[0]
    kf = filt[:, :, 0].at[0].add(filt[0, :, 1])
    kb = filt[:, :, 1].at[0].set(0.0)
    k_all = jnp.concatenate([kf, kb], axis=1).reshape(L, 2 * HYENA_ORDER * W_H)
    f_hi, f_lo = _split2(fwd)
    k_hi, k_lo = _split2(k_all)
    mm = functools.partial(matmul, tm=2 * FBLK, tn=512)
    spec = mm(f_hi, k_lo) + mm(f_lo, k_hi) + mm(f_hi, k_hi)
    spec = spec.reshape(-1, 2 * HYENA_ORDER, W_H)
    sf, sb = spec[:, :HYENA_ORDER], spec[:, HYENA_ORDER:]
    is_im = ((jnp.arange(spec.shape[0]) // FBLK) % 2 == 1)[:, None, None]
    ks = sf + jnp.where(is_im, -sb, sb)
    return [ks[:, o] for o in range(HYENA_ORDER)]


def hyena_pallas(z, conv_w, filt, bias, mats, B, L):
    fwd, inv = mats
    fwd_b, inv_b = fwd.astype(BF16), inv.astype(BF16)
    k0, k1 = hyena_filter_spectra(filt, fwd)
    v32, vb, x1, x2 = hyena_short_conv(z, conv_w, B, L)
    b0 = bias[0].reshape(1, W_H).astype(F32)
    b1 = bias[1].reshape(1, W_H).astype(F32)
    spec = dft_filter_fwd(fwd_b, vb, k0, B, L)
    s32, sb = dft_inv_gate(inv_b, spec, x1, v32, b0, B, L, (F32, BF16))
    spec = dft_filter_fwd(fwd_b, sb, k1, B, L)
    (y,) = dft_inv_gate(inv_b, spec, x2, s32, b1, B, L, (BF16,))
    return y


def _norm_mod_kernel(x_ref, g_ref, sc_ref, sh_ref, o_ref):
    x = x_ref[...]
    y = x * lax.rsqrt(jnp.mean(x * x, axis=-1, keepdims=True) + EPS)
    o_ref[...] = ((y * g_ref[...]) * (1.0 + sc_ref[0]) + sh_ref[0]).astype(o_ref.dtype)


def norm_mod(x, g, sc, sh, L, out_dtype=BF16):
    T = x.shape[0]
    tm = 256
    per_row = sc.shape[0] > 1
    mod = pl.BlockSpec((1, 1, D_MODEL), (lambda i: (i // (L // tm), 0, 0)) if per_row else (lambda i: (0, 0, 0)))
    return pl.pallas_call(
        _norm_mod_kernel,
        grid=(T // tm,),
        in_specs=[pl.BlockSpec((tm, D_MODEL), lambda i: (i, 0)), pl.BlockSpec((1, D_MODEL), lambda i: (0, 0)),
                  mod, mod],
        out_specs=pl.BlockSpec((tm, D_MODEL), lambda i: (i, 0)),
        out_shape=jax.ShapeDtypeStruct((T, D_MODEL), out_dtype),
        compiler_params=pltpu.CompilerParams(dimension_semantics=("parallel",), vmem_limit_bytes=VMEM_LIMIT_BYTES),
    )(x, g.reshape(1, D_MODEL), sc, sh)


def _branch_mix_kernel(ya_ref, yb_ref, yc_ref, wb_ref, za_ref, zb_ref, zc_ref, o_ref):
    acc = None
    for i, (y_ref, zg_ref) in enumerate(((ya_ref, za_ref), (yb_ref, zb_ref), (yc_ref, zc_ref))):
        t = jax.nn.sigmoid(zg_ref[...]) * jnp.dot(y_ref[...], wb_ref[i], preferred_element_type=F32)
        acc = t if acc is None else acc + t
    o_ref[...] = acc.astype(o_ref.dtype)


Z_COL_ZM = (3 * W_H + 2 * H_M * DK_M + 2 * H_M * DV_M + H_G * (2 * DK_G + DV_G) + H_G * DV_G)


def branch_mix(ya, yb, yc, wb, z):
    T = ya.shape[0]
    tm, tn = 512, 512
    nj = D_MODEL // tn
    yspec = pl.BlockSpec((tm, BRANCH_W), lambda i, j: (i, 0))
    zspec = lambda br: pl.BlockSpec((tm, tn), lambda i, j: (i, Z_COL_ZM // tn + br * nj + j))
    return pl.pallas_call(
        _branch_mix_kernel,
        grid=(T // tm, nj),
        in_specs=[yspec, yspec, yspec, pl.BlockSpec((N_BRANCH, BRANCH_W, tn), lambda i, j: (0, 0, j)),
                  zspec(0), zspec(1), zspec(2)],
        out_specs=pl.BlockSpec((tm, tn), lambda i, j: (i, j)),
        out_shape=jax.ShapeDtypeStruct((T, D_MODEL), BF16),
        compiler_params=pltpu.CompilerParams(
            dimension_semantics=("parallel", "parallel"), vmem_limit_bytes=VMEM_LIMIT_BYTES),
    )(ya, yb, yc, wb, z, z, z)


def _mm_res_kernel(a_ref, w_ref, r_ref, g_ref, o_ref, *, nk):
    part = jnp.dot(a_ref[...], w_ref[...], preferred_element_type=F32)
    if nk == 1:
        o_ref[...] = r_ref[...] + g_ref[0] * part
    else:
        k = pl.program_id(2)

        @pl.when(k == 0)
        def _():
            o_ref[...] = part

        @pl.when((k != 0) & (k != nk - 1))
        def _():
            o_ref[...] += part

        @pl.when(k == nk - 1)
        def _():
            o_ref[...] = r_ref[...] + g_ref[0] * (o_ref[...] + part)


def matmul_residual(a, w, res, gate, L, *, tm, tn, tk=None):
    T, K = a.shape
    N = w.shape[1]
    tk = K if tk is None else tk
    nk = K // tk
    assert T % tm == 0 and N % tn == 0 and K % tk == 0 and L % tm == 0, (T, N, K, L, tm, tn, tk)
    per_row = gate.shape[0] > 1
    gspec = pl.BlockSpec((1, 1, tn), (lambda i, j, k: (i // (L // tm), 0, j)) if per_row else (lambda i, j, k: (0, 0, j)))
    return pl.pallas_call(
        functools.partial(_mm_res_kernel, nk=nk),
        grid=(T // tm, N // tn, nk),
        in_specs=[pl.BlockSpec((tm, tk), lambda i, j, k: (i, k)), pl.BlockSpec((tk, tn), lambda i, j, k: (k, j)),
                  pl.BlockSpec((tm, tn), lambda i, j, k: (i, j)), gspec],
        out_specs=pl.BlockSpec((tm, tn), lambda i, j, k: (i, j)),
        out_shape=jax.ShapeDtypeStruct((T, N), F32),
        compiler_params=pltpu.CompilerParams(
            dimension_semantics=("parallel", "parallel", "arbitrary"), vmem_limit_bytes=VMEM_LIMIT_BYTES),
    )(a, w, res, gate)


FFN_ROWS = 256
FFN_HALO = 128
FFN_CT = 256


def _ffn_act_kernel(gp_ref, up_ref, w_ref, b_ref, o_ref, *, L, rows):
    width = L // rows
    wshift = width.bit_length() - 1
    n_ext = FFN_ROWS + 2 * FFN_HALO

    def body(c, carry):
        r0 = pl.multiple_of(c * FFN_ROWS, FFN_ROWS)
        p0 = pl.multiple_of(jnp.maximum(r0 - FFN_HALO, 0), FFN_HALO)
        n0 = pl.multiple_of(jnp.minimum(r0 + FFN_ROWS, L - FFN_HALO), FFN_HALO)
        ext = jnp.concatenate([gp_ref[pl.ds(p0, FFN_HALO), :], gp_ref[pl.ds(r0, FFN_ROWS), :],
                               gp_ref[pl.ds(n0, FFN_HALO), :]], axis=0)
        tok = r0 + lax.broadcasted_iota(jnp.int32, (FFN_ROWS, 1), 0)
        gr = tok >> wshift
        gw = tok & (width - 1)
        acc = jnp.zeros((FFN_ROWS, FFN_CT), F32) + b_ref[...]
        for i in range(3):
            for j in range(3):
                dr, dw = i - 1, j - 1
                s = dr * width + dw
                tap = ext if s == 0 else pltpu.roll(ext, (-s) % n_ext, 0)
                tap = tap[FFN_HALO:FFN_HALO + FFN_ROWS]
                ok = (gr + dr >= 0) & (gr + dr < rows) & (gw + dw >= 0) & (gw + dw < width)
                acc = acc + jnp.where(ok, tap, 0.0) * w_ref[3 * i + j:3 * i + j + 1, :]
        r = pl.ds(r0, FFN_ROWS)
        o_ref[r, :] = ((acc * jax.nn.sigmoid(acc)) * up_ref[r, :]).astype(o_ref.dtype)
        return carry

    lax.fori_loop(0, L // FFN_ROWS, body, 0)


def ffn_act(gate_pre, up, conv_w, conv_b, B, L, rows):
    C = gate_pre.shape[1]
    blk = pl.BlockSpec((L, FFN_CT), lambda b, j: (b, j))
    return pl.pallas_call(
        functools.partial(_ffn_act_kernel, L=L, rows=rows),
        grid=(B, C // FFN_CT),
        in_specs=[blk, blk, pl.BlockSpec((9, FFN_CT), lambda b, j: (0, j)),
                  pl.BlockSpec((1, FFN_CT), lambda b, j: (0, j))],
        out_specs=blk,
        out_shape=jax.ShapeDtypeStruct((B * L, C), BF16),
        compiler_params=pltpu.CompilerParams(
            dimension_semantics=("parallel", "parallel"), vmem_limit_bytes=VMEM_LIMIT_BYTES),
    )(gate_pre, up, conv_w.reshape(9, C), conv_b.reshape(1, C))


def hyena_filters(L, w1, b1, w2, b2, freq, w3):
    t = jnp.arange(L, dtype=F32) / L
    bands = jnp.linspace(1e-4, FILTER_BANDS - 1, FILTER_BANDS, dtype=F32)
    ang = 2.0 * math.pi * t[:, None] * bands[None, :]
    feat = jnp.concatenate([t[:, None], jnp.cos(ang), jnp.sin(ang)], axis=-1)
    fr = freq.astype(F32)
    hid = jnp.sin(fr * (feat @ w1.astype(F32) + b1.astype(F32)))
    hid = jnp.sin(fr * (hid @ w2.astype(F32) + b2.astype(F32)))
    h = (hid @ w3.astype(F32)).reshape(L, HYENA_ORDER, 2, W_H)
    deltas = jnp.abs(jnp.linspace(math.log(1e-2) / 1.5, math.log(1e-2) / 0.3, W_H, dtype=F32))
    h = h * jnp.exp(-t[:, None] * deltas[None, :])[:, None, None, :]
    return h * lax.rsqrt(jnp.sum(h * h, axis=(0, 2), keepdims=True) + EPS)


def trunk_layer(x, mod, p, B, L, rows, states):
    sh1, sc1, g1, sh2, sc2, g2 = mod
    mC, mn, mm, gS = states
    hb = norm_mod(x, p['norm_mix'], sc1, sh1, L)
    z = matmul(hb, p['w_in_main'], tm=1024, tn=512)
    zs = matmul(hb, p['w_in_small'], tm=1024, tn=N_SMALL_PAD)
    filt = hyena_filters(L, p['hyena_w1'], p['hyena_b1'], p['hyena_w2'], p['hyena_b2'], p['hyena_freq'], p['hyena_w3'])
    y_a = hyena_pallas(z, p['hyena_conv_w'], filt, p['hyena_bias'], p['dft'][L], B, L)
    y_b, m_state = mlstm_pallas(z, zs, p['mlstm_gate_b'], p['mlstm_norm'],
                                None if mC is None else (mC, mn, mm), B, L)
    y_c, g_state = gdn_pallas(z, zs, p['gdn_conv_w'], p['gdn_A_log'], p['gdn_dt_bias'], p['gdn_norm'], gS, B, L)
    mix = branch_mix(y_a, y_b, y_c, p['w_branch'], z)
    x = matmul_residual(mix, p['w_out'], x, g1, L, tm=min(1024, L), tn=512)
    hb = norm_mod(x, p['norm_ffn'], sc2, sh2, L)
    gate_pre = matmul(hb, p['ffn_w_gate'], tm=1024, tn=1024)
    up = matmul(hb, p['ffn_w_up'], tm=1024, tn=1024)
    act = ffn_act(gate_pre, up, p['ffn_conv_w'], p['ffn_conv_b'], B, L, rows)
    x = matmul_residual(act, p['ffn_w_down'], x, g2, L, tm=min(1024, L), tn=512, tk=D_FF_PAD // 2)
    return x, m_state + (g_state,)


def _ada_all(c, c_ctx, ada_w, ada_b):
    cc = jnp.concatenate([c, c_ctx[None, :], jnp.zeros((7, D_MODEL), F32)], axis=0)
    a = jax.nn.silu(cc)
    outs = []
    for l in range(DEPTH):
        outs.append(matmul(a, ada_w[l], tm=16, tn=1024) + ada_b[l])
    return outs


def kernel(x_prompt, x_sample, state_mlstm_C, state_mlstm_n, state_mlstm_m, state_gdn_S, c, c_ctx,
           ada_w, ada_b, norm_mix, norm_ffn, w_in, hyena_conv_w, hyena_w1, hyena_b1, hyena_w2, hyena_b2,
           hyena_freq, hyena_w3, hyena_bias, mlstm_gate_b, mlstm_norm, gdn_conv_w, gdn_A_log, gdn_dt_bias,
           gdn_norm, w_branch, w_out, ffn_w_gate, ffn_w_up, ffn_conv_w, ffn_conv_b, ffn_w_down, final_norm):
    bp, lp, _ = x_prompt.shape
    nb, ls, _ = x_sample.shape
    rows_lat = ls // GRID_W
    zero_states = (None, None, None, None)
    ada = _ada_all(c, c_ctx, ada_w, ada_b)
    ffpad = D_FF_PAD - D_FF
    dft = {L: dft_matrices(L) for L in sorted({lp, ls})}
    xp, xs = x_prompt.reshape(bp * lp, D_MODEL), x_sample.reshape(nb * ls, D_MODEL)
    out_C, out_n, out_m, out_S = [], [], [], []
    for l in range(DEPTH):
        wl = w_in[l]
        w_main = jnp.concatenate([wl[:, :_OFF[5]], wl[:, _OFF[6]:_OFF[8]], wl[:, _OFF[10]:]], axis=1).astype(BF16)
        w_small = jnp.concatenate([wl[:, _OFF[5]:_OFF[6]], wl[:, _OFF[8]:_OFF[10]],
                                   jnp.zeros((D_MODEL, N_SMALL_PAD - 96), F32)], axis=1).astype(BF16)
        p = {'dft': dft, 'norm_mix': norm_mix[l], 'norm_ffn': norm_ffn[l], 'w_in_main': w_main, 'w_in_small': w_small,
             'hyena_conv_w': hyena_conv_w[l], 'hyena_w1': hyena_w1[l], 'hyena_b1': hyena_b1[l],
             'hyena_w2': hyena_w2[l], 'hyena_b2': hyena_b2[l], 'hyena_freq': hyena_freq[l],
             'hyena_w3': hyena_w3[l], 'hyena_bias': hyena_bias[l],
             'mlstm_gate_b': mlstm_gate_b[l], 'mlstm_norm': mlstm_norm[l],
             'gdn_conv_w': gdn_conv_w[l], 'gdn_A_log': gdn_A_log[l], 'gdn_dt_bias': gdn_dt_bias[l],
             'gdn_norm': gdn_norm[l], 'w_branch': w_branch[l].astype(BF16), 'w_out': w_out[l].astype(BF16),
             'ffn_w_gate': jnp.pad(ffn_w_gate[l], ((0, 0), (0, ffpad))).astype(BF16),
             'ffn_w_up': jnp.pad(ffn_w_up[l], ((0, 0), (0, ffpad))).astype(BF16),
             'ffn_conv_w': jnp.pad(ffn_conv_w[l], ((0, 0), (0, 0), (0, ffpad))),
             'ffn_conv_b': jnp.pad(ffn_conv_b[l], ((0, ffpad),)),
             'ffn_w_down': jnp.pad(ffn_w_down[l], ((0, ffpad), (0, 0))).astype(BF16)}
        m = ada[l]
        mod_ctx = [t[None, None, :] for t in jnp.split(m[nb], 6, axis=-1)]
        mod_lat = [t[:, None, :] for t in jnp.split(m[:nb], 6, axis=-1)]
        xp, (sC, sn, sm, sS) = trunk_layer(xp, mod_ctx, p, bp, lp, 1, zero_states)
        out_C.append(sC)
        out_n.append(sn)
        out_m.append(sm)
        out_S.append(sS)
        cached = (state_mlstm_C[:, l].astype(F32), state_mlstm_n[:, l].astype(F32),
                  state_mlstm_m[:, l].astype(F32), state_gdn_S[:, l].astype(F32))
        xs, _ = trunk_layer(xs, mod_lat, p, nb, ls, rows_lat, cached)
    zero = jnp.zeros((1, 1, D_MODEL), F32)
    y_prompt = norm_mod(xp, final_norm, zero, zero, lp, out_dtype=F32).reshape(bp, lp, D_MODEL)
    y_sample = norm_mod(xs, final_norm, zero, zero, ls, out_dtype=F32).reshape(nb, ls, D_MODEL)
    return (y_prompt, y_sample, jnp.stack(out_C, axis=1), jnp.stack(out_n, axis=1),
            jnp.stack(out_m, axis=1), jnp.stack(out_S, axis=1))
```

```python
import functools
import math

import jax
import jax.numpy as jnp
import numpy as np
from jax import lax
from jax.experimental import pallas as pl
from jax.experimental.pallas import tpu as pltpu

D_MODEL = 4096
DEPTH = 2
GRID_W = 64
N_BRANCH = 3
BRANCH_W = D_MODEL // 2
W_H = BRANCH_W
HYENA_ORDER = 2
FILTER_BANDS = 16
H_M = 8
DV_M = BRANCH_W // H_M
DK_M = DV_M // 2
H_G = 16
DK_G = BRANCH_W // H_G
DV_G = BRANCH_W // H_G
D_FF = 11008
D_FF_PAD = 11264
CHUNK = 64
EPS = 1e-6
F32 = jnp.float32
BF16 = jnp.bfloat16

_IN_SPLITS = (3 * W_H, H_M * DK_M, H_M * DK_M, H_M * DV_M, H_M * DV_M, 4 * H_M,
              H_G * (2 * DK_G + DV_G), H_G * DV_G, 2 * H_G, 2 * H_G, N_BRANCH * D_MODEL)
_OFF = np.concatenate([[0], np.cumsum(_IN_SPLITS)]).tolist()
N_SMALL_PAD = 128

VMEM_LIMIT_BYTES = 56 * 1024 * 1024


def _mm_kernel(x_ref, w_ref, o_ref, *, nk):
    part = jnp.dot(x_ref[...].astype(BF16), w_ref[...].astype(BF16), preferred_element_type=F32)
    if nk == 1:
        o_ref[...] = part
    else:
        k = pl.program_id(2)

        @pl.when(k == 0)
        def _():
            o_ref[...] = part

        @pl.when(k != 0)
        def _():
            o_ref[...] += part


def matmul(x, w, *, tm, tn, tk=None):
    M, K = x.shape
    K2, N = w.shape
    assert K == K2
    tk = K if tk is None else tk
    tm = min(tm, M)
    assert M % tm == 0 and N % tn == 0 and K % tk == 0, (M, N, K, tm, tn, tk)
    nk = K // tk
    return pl.pallas_call(
        functools.partial(_mm_kernel, nk=nk),
        grid=(M // tm, N // tn, nk),
        in_specs=[pl.BlockSpec((tm, tk), lambda i, j, k: (i, k)),
                  pl.BlockSpec((tk, tn), lambda i, j, k: (k, j))],
        out_specs=pl.BlockSpec((tm, tn), lambda i, j, k: (i, j)),
        out_shape=jax.ShapeDtypeStruct((M, N), F32),
        compiler_params=pltpu.CompilerParams(
            dimension_semantics=("parallel", "parallel", "arbitrary"),
            vmem_limit_bytes=VMEM_LIMIT_BYTES),
    )(x, w)


def _dot(a, b):
    return jnp.dot(a.astype(BF16), b.astype(BF16), preferred_element_type=F32)


def _dot_nt(a, b):
    return lax.dot_general(a.astype(BF16), b.astype(BF16), (((1,), (1,)), ((), ())), preferred_element_type=F32)


def _dot_tn(a, b):
    return lax.dot_general(a.astype(BF16), b.astype(BF16), (((0,), (0,)), ((), ())), preferred_element_type=F32)


def _split2(a):
    hi = a.astype(BF16)
    return hi, (a - hi.astype(F32)).astype(BF16)


def _split3(a):
    hi = a.astype(BF16)
    r = a - hi.astype(F32)
    mid = r.astype(BF16)
    return hi, mid, (r - mid.astype(F32)).astype(BF16)


def _dot3(a, b):
    ah, al = _split2(a)
    bh, bl = _split2(b)
    return (jnp.dot(ah, bl, preferred_element_type=F32) + jnp.dot(al, bh, preferred_element_type=F32)
            + jnp.dot(ah, bh, preferred_element_type=F32))


def _dot_exact_lhs(a_bf16, b):
    b0, b1, b2 = _split3(b)
    return (jnp.dot(a_bf16, b2, preferred_element_type=F32) + jnp.dot(a_bf16, b1, preferred_element_type=F32)
            + jnp.dot(a_bf16, b0, preferred_element_type=F32))


def _inv_unit_triangular(ns, eye):
    xs = [eye - n for n in ns]
    ps = list(ns)
    for _ in range(5):
        ps = [_dot(p, p) for p in ps]
        xs = [x + _dot(x, p) for x, p in zip(xs, ps)]
    rs = [eye - x - _dot3(n, x) for n, x in zip(ns, xs)]
    return [x + _dot(x, r) for x, r in zip(xs, rs)]


def _softplus(x):
    return jnp.maximum(x, 0.0) + jnp.log1p(jnp.exp(-jnp.abs(x)))


def _lane_pick(x, lane):
    idx = lax.broadcasted_iota(jnp.int32, x.shape, 1)
    return jnp.sum(jnp.where(idx == lane, x, 0.0), axis=-1, keepdims=True)


def _conv3_rows(ref, w_ref, r0, nrows, total):
    x = ref[pl.ds(r0, nrows), :]
    row = lax.broadcasted_iota(jnp.int32, x.shape, 0)
    p0 = jnp.maximum(r0 - 8, 0)
    n0 = jnp.minimum(r0 + nrows, total - 8)
    prev = ref[pl.ds(pl.multiple_of(p0, 8), 8), :][7:8, :] * (r0 > 0).astype(F32)
    nxt = ref[pl.ds(pl.multiple_of(n0, 8), 8), :][0:1, :] * (r0 + nrows < total).astype(F32)
    xm1 = jnp.where(row == 0, prev, pltpu.roll(x, 1, 0))
    xp1 = jnp.where(row == nrows - 1, nxt, pltpu.roll(x, nrows - 1, 0))
    return xm1 * w_ref[0:1, :] + x * w_ref[1:2, :] + xp1 * w_ref[2:3, :]


GDN_LANE_BETA = 32
GDN_LANE_DECAY = 64
PREP_ROWS = 256
GDN_GROUP = 8


def _gdn_kernel(zq_ref, zk_ref, zv_ref, gz_ref, zs_ref, cwq_ref, cwk_ref, cwv_ref, alog_ref, dtb_ref, ng_ref,
                s0_ref, y_ref, sout_ref,
                q_s, k_s, v_s, gall_s, ball_s, o_s, wq_s, u_s, qkd_s, kd_s, eend_s, st_s, *, L, zero_init):
    h = pl.program_id(1)
    nc = L // CHUNK
    T = CHUNK
    row = lax.broadcasted_iota(jnp.int32, (T, T), 0)
    col = lax.broadcasted_iota(jnp.int32, (T, T), 1)
    eye = (row == col).astype(F32)
    ones_b = jnp.ones((T, T), BF16)
    incl = (row >= col, row <= col)
    strict = (row > col, row < col)

    @pl.when(h == 0)
    def _():
        lane = lax.broadcasted_iota(jnp.int32, (T, 128), 1)
        tril_b = incl[0].astype(BF16)
        triu_b = incl[1].astype(BF16)

        def seg(c, carry):
            r = pl.ds(pl.multiple_of(c * T, T), T)
            zs = zs_ref[r, :]
            ball_s[r, :] = jax.nn.sigmoid(zs)
            g = -jnp.exp(alog_ref[...]) * _softplus(zs + dtb_ref[...])
            pre = _dot_exact_lhs(tril_b, g)
            suf = _dot_exact_lhs(triu_b, g)
            gall_s[r, :] = jnp.where(lane < GDN_LANE_DECAY + H_G, pre, suf)
            return carry

        lax.fori_loop(0, nc, seg, 0)

    def prep(i, carry):
        r0 = pl.multiple_of(i * PREP_ROWS, PREP_ROWS)
        r = pl.ds(r0, PREP_ROWS)
        q = _conv3_rows(zq_ref, cwq_ref, r0, PREP_ROWS, L)
        q = q * jax.nn.sigmoid(q)
        q = q * lax.rsqrt(jnp.sum(q * q, axis=-1, keepdims=True) + EPS) * DK_G ** -0.5
        q_s[r, :] = q.astype(BF16)
        o_s[r, :] = jnp.zeros((PREP_ROWS, 128), F32)
        k = _conv3_rows(zk_ref, cwk_ref, r0, PREP_ROWS, L)
        k = k * jax.nn.sigmoid(k)
        k_s[r, :] = k * lax.rsqrt(jnp.sum(k * k, axis=-1, keepdims=True) + EPS)
        v = _conv3_rows(zv_ref, cwv_ref, r0, PREP_ROWS, L)
        v_s[r, :] = v * jax.nn.sigmoid(v)
        return carry

    lax.fori_loop(0, L // PREP_ROWS, prep, 0)

    grp = min(GDN_GROUP, nc)

    def phase_a(gi, carry):
        cs = [gi * grp + j for j in range(grp)]
        rs = [pl.ds(pl.multiple_of(c * T, T), T) for c in cs]
        qs = [q_s[r, :] for r in rs]
        ks = [k_s[r, :] for r in rs]
        vs = [v_s[r, :] for r in rs]
        kks = [_dot_nt(k, k) for k in ks]
        qks = [_dot_nt(q, k) for q, k in zip(qs, ks)]
        items = [(j, d) for j in range(grp) for d in range(2)]
        gcs = [_lane_pick(gall_s[rs[j], :], GDN_LANE_DECAY + d * H_G + h) for j, d in items]
        bcs = [_lane_pick(ball_s[rs[j], :], GDN_LANE_BETA + d * H_G + h) for j, d in items]
        gsq = [jnp.broadcast_to(g, (T, T)) for g in gcs]
        grs = [_dot_exact_lhs(ones_b, jnp.where(row == col, g, 0.0)) for g in gsq]
        decs = [jnp.exp(jnp.where(incl[d], g - gr, -1e30)) for (j, d), g, gr in zip(items, gsq, grs)]
        ns = [jnp.where(strict[d], b * kks[j] * dec, 0.0) for (j, d), b, dec in zip(items, bcs, decs)]
        tinvs = _inv_unit_triangular(ns, eye)
        egs = [jnp.exp(g) for g in gcs]
        ws = [_dot3(t, (b * eg) * ks[j]) for (j, d), t, b, eg in zip(items, tinvs, bcs, egs)]
        us = [_dot3(t, b * vs[j]) for (j, d), t, b in zip(items, tinvs, bcs)]
        for idx, (j, d) in enumerate(items):
            c, r, gc = cs[j], rs[j], gcs[idx]
            g_end = gc[T - 1:T, :] if d == 0 else gc[0:1, :]
            wq_s[d, pl.ds(pl.multiple_of(c * 2 * T, 2 * T), T), :] = ws[idx].astype(BF16)
            wq_s[d, pl.ds(pl.multiple_of(c * 2 * T, 2 * T) + T, T), :] = (egs[idx] * qs[j]).astype(BF16)
            u_s[d, r, :] = us[idx]
            qkd_s[d, r, :] = (qks[j] * decs[idx]).astype(BF16)
            kd_s[d, r, :] = (jnp.exp(g_end - gc) * ks[j]).astype(BF16)
            eend_s[d, pl.ds(pl.multiple_of(c * 8, 8), 8), :] = jnp.broadcast_to(jnp.exp(g_end), (8, 128))
        return carry

    lax.fori_loop(0, nc // grp, phase_a, 0)

    for d in range(2):
        if zero_init:
            st_s[d] = jnp.zeros((DK_G, DV_G), F32)
        else:
            st_s[d] = s0_ref[0, d, 0]

    def phase_b(i, carry):
        cs = (i, nc - 1 - i)
        rs = [pl.ds(pl.multiple_of(c * T, T), T) for c in cs]
        ss = [st_s[d] for d in range(2)]
        wss = [jnp.dot(wq_s[d, pl.ds(pl.multiple_of(cs[d] * 2 * T, 2 * T), 2 * T), :], ss[d].astype(BF16),
                       preferred_element_type=F32) for d in range(2)]
        vns = [(u_s[d, rs[d], :] - wss[d][:T]).astype(BF16) for d in range(2)]
        os_ = [jnp.dot(qkd_s[d, rs[d], :], vns[d], preferred_element_type=F32) for d in range(2)]
        kvs = [_dot_tn(kd_s[d, rs[d], :], vns[d]) for d in range(2)]
        for d in range(2):
            e_end = eend_s[d, pl.ds(pl.multiple_of(cs[d] * 8, 8), 8), :][0:1, :]
            st_s[d] = e_end * ss[d] + kvs[d]
        for d in range(2):
            o_s[rs[d], :] += wss[d][T:] + os_[d]
        return carry

    lax.fori_loop(0, nc, phase_b, 0)
    for d in range(2):
        sout_ref[0, d, 0] = st_s[d]

    def fin(i, carry):
        r = pl.ds(pl.multiple_of(i * PREP_ROWS, PREP_ROWS), PREP_ROWS)
        o = o_s[r, :]
        o = o * lax.rsqrt(jnp.mean(o * o, axis=-1, keepdims=True) + EPS) * ng_ref[...]
        z = gz_ref[r, :]
        y_ref[r, :] = (o * (z * jax.nn.sigmoid(z))).astype(y_ref.dtype)
        return carry

    lax.fori_loop(0, L // PREP_ROWS, fin, 0)


Z_COL_GQ = (3 * W_H + 2 * H_M * DK_M + 2 * H_M * DV_M) // 128
Z_COL_GZ = Z_COL_GQ + 3 * H_G


def gdn_pallas(z, zs, conv_w, a_log, dt_bias, norm_g, s0, B, L):
    zero_init = s0 is None
    if zero_init:
        s0 = jnp.zeros((1, 2, 1, DK_G, DV_G), F32)
        s0_map = lambda b, h: (0, 0, 0, 0, 0)
    else:
        s0_map = lambda b, h: (b, 0, h, 0, 0)
    lanes = jnp.zeros((1, 128), F32)
    alog_row = lanes.at[0, GDN_LANE_DECAY:GDN_LANE_DECAY + 2 * H_G].set(a_log.reshape(-1))
    dtb_row = lanes.at[0, GDN_LANE_DECAY:GDN_LANE_DECAY + 2 * H_G].set(dt_bias.reshape(-1))
    nc = L // CHUNK
    col = lambda off: pl.BlockSpec((L, 128), lambda b, h: (b, off + h))
    cw = lambda off: pl.BlockSpec((3, 128), lambda b, h: (0, off + h))
    one = pl.BlockSpec((1, 128), lambda b, h: (0, 0))
    st = lambda m: pl.BlockSpec((1, 2, 1, DK_G, DV_G), m)
    return pl.pallas_call(
        functools.partial(_gdn_kernel, L=L, zero_init=zero_init),
        grid=(B, H_G),
        in_specs=[col(Z_COL_GQ), col(Z_COL_GQ + H_G), col(Z_COL_GQ + 2 * H_G), col(Z_COL_GZ),
                  pl.BlockSpec((L, 128), lambda b, h: (b, 0)),
                  cw(0), cw(H_G), cw(2 * H_G), one, one, one, st(s0_map)],
        out_specs=[pl.BlockSpec((L, 128), lambda b, h: (b, h)), st(lambda b, h: (b, 0, h, 0, 0))],
        out_shape=[jax.ShapeDtypeStruct((B * L, H_G * DV_G), BF16),
                   jax.ShapeDtypeStruct((B, 2, H_G, DK_G, DV_G), F32)],
        scratch_shapes=[pltpu.VMEM((L, 128), BF16), pltpu.VMEM((L, 128), F32), pltpu.VMEM((L, 128), F32),
                        pltpu.VMEM((L, 128), F32), pltpu.VMEM((L, 128), F32), pltpu.VMEM((L, 128), F32),
                        pltpu.VMEM((2, 2 * L, 128), BF16), pltpu.VMEM((2, L, 128), F32),
                        pltpu.VMEM((2, L, CHUNK), BF16), pltpu.VMEM((2, L, 128), BF16),
                        pltpu.VMEM((2, nc * 8, 128), F32), pltpu.VMEM((2, DK_G, DV_G), F32)],
        compiler_params=pltpu.CompilerParams(
            dimension_semantics=("parallel", "arbitrary"), vmem_limit_bytes=VMEM_LIMIT_BYTES),
    )(z, z, z, z, zs, conv_w, conv_w, conv_w, alog_row, dtb_row, norm_g.reshape(1, DV_G), s0)


MLSTM_LANE_I = 0
MLSTM_LANE_F = 2 * H_M
NEG_BIG = -1e30


def _mlstm_kernel(q_ref, k_ref, v_ref, mo_ref, zs_ref, gb_ref, ng_ref, c0_ref, n0_ref, m0_ref,
                  y_ref, cout_ref, nout_ref, mout_ref, gall_s, h_s, c_s, n_s, m_s, *, L, zero_init):
    h = pl.program_id(1)
    nc = L // CHUNK
    T = CHUNK
    row = lax.broadcasted_iota(jnp.int32, (T, T), 0)
    col = lax.broadcasted_iota(jnp.int32, (T, T), 1)
    ones_b = jnp.ones((T, T), BF16)
    incl = (row >= col, row <= col)

    @pl.when(h == 0)
    def _():
        lane = lax.broadcasted_iota(jnp.int32, (T, 128), 1)
        tril_b = incl[0].astype(BF16)
        triu_b = incl[1].astype(BF16)

        def seg(c, carry):
            r = pl.ds(pl.multiple_of(c * T, T), T)
            gp = zs_ref[r, :] + gb_ref[...]
            lf = jnp.minimum(gp, 0.0) - jnp.log1p(jnp.exp(-jnp.abs(gp)))
            pre = _dot_exact_lhs(tril_b, lf)
            suf = _dot_exact_lhs(triu_b, lf)
            gall_s[r, :] = jnp.where(lane < MLSTM_LANE_F, gp, jnp.where(lane < MLSTM_LANE_F + H_M, pre, suf))
            return carry

        lax.fori_loop(0, nc, seg, 0)

    for d in range(2):
        if zero_init:
            c_s[d] = jnp.zeros((DK_M, DV_M), F32)
            n_s[d] = jnp.zeros((8, DK_M), F32)
            m_s[d] = jnp.zeros((8, 128), F32)
        else:
            c_s[d] = c0_ref[0, d, 0]
            n_s[d] = jnp.broadcast_to(n0_ref[0, d, 0], (8, DK_M))
            m_s[d] = jnp.broadcast_to(m0_ref[0, d, 0], (8, 128))

    def zero_h(i, carry):
        h_s[pl.ds(pl.multiple_of(i * PREP_ROWS, PREP_ROWS), PREP_ROWS), :] = jnp.zeros((PREP_ROWS, DV_M), F32)
        return carry

    lax.fori_loop(0, L // PREP_ROWS, zero_h, 0)

    def step(i, carry):
        D = range(2)
        cs = (i, nc - 1 - i)
        rs = [pl.ds(pl.multiple_of(c * T, T), T) for c in cs]
        qs = [q_ref[r, :] for r in rs]
        ks = [k_ref[r, :] * DK_M ** -0.5 for r in rs]
        vs = [v_ref[r, :].astype(BF16) for r in rs]
        qbs = [q.astype(BF16) for q in qs]
        qks = [_dot_nt(qbs[d], ks[d]) for d in D]
        cst = [c_s[d] for d in D]
        qcs = [jnp.dot(qbs[d], cst[d].astype(BF16), preferred_element_type=F32) for d in D]
        bcol = [_lane_pick(gall_s[rs[d], :], MLSTM_LANE_F + d * H_M + h) for d in D]
        icol = [_lane_pick(gall_s[rs[d], :], MLSTM_LANE_I + d * H_M + h) for d in D]
        amb = [icol[d] - bcol[d] for d in D]
        rmat = [_dot_exact_lhs(ones_b, jnp.where(row == col, jnp.broadcast_to(amb[d], (T, T)), 0.0)) for d in D]
        dmat = [jnp.where(incl[d], bcol[d] + rmat[d], NEG_BIG) for d in D]
        m = [m_s[d][0:1, 0:1] for d in D]
        inter = [bcol[d] + m[d] for d in D]
        m_t = [jnp.maximum(inter[d], jnp.max(dmat[d], axis=-1, keepdims=True)) for d in D]
        smat = [qks[d] * jnp.exp(dmat[d] - m_t[d]) for d in D]
        w_inter = [jnp.exp(inter[d] - m_t[d]) for d in D]
        sv = [jnp.dot(smat[d].astype(BF16), vs[d], preferred_element_type=F32) for d in D]
        b_end = [bcol[0][T - 1:T, :], bcol[1][0:1, :]]
        gcol = [b_end[d] + amb[d] for d in D]
        m_new = [jnp.maximum(b_end[d] + m[d], jnp.max(gcol[d], axis=0, keepdims=True)) for d in D]
        kw = [jnp.exp(gcol[d] - m_new[d]) * ks[d] for d in D]
        kv = [_dot_tn(kw[d], vs[d]) for d in D]
        for d in D:
            n = n_s[d][0:1, :]
            cd = jnp.exp(b_end[d] + m[d] - m_new[d])
            num = sv[d] + w_inter[d] * qcs[d]
            den = (jnp.sum(smat[d], axis=-1, keepdims=True)
                   + w_inter[d] * jnp.sum(qs[d] * n, axis=-1, keepdims=True))
            h_s[rs[d], :] += num / jnp.maximum(jnp.abs(den), jnp.exp(-m_t[d]))
            c_s[d] = cd * cst[d] + kv[d]
            n_s[d] = jnp.broadcast_to(cd * n + jnp.sum(kw[d], axis=0, keepdims=True), (8, DK_M))
            m_s[d] = jnp.broadcast_to(m_new[d], (8, 128))
        return carry

    lax.fori_loop(0, nc, step, 0)
    for d in range(2):
        cout_ref[0, d, 0] = c_s[d]
        nout_ref[0, d, 0] = n_s[d][0:1, :]
        mout_ref[0, d, 0] = m_s[d][0:1, 0:1]

    def fin(i, carry):
        r = pl.ds(pl.multiple_of(i * PREP_ROWS, PREP_ROWS), PREP_ROWS)
        o = h_s[r, :]
        o = o * lax.rsqrt(jnp.mean(o * o, axis=-1, keepdims=True) + EPS) * ng_ref[0]
        y_ref[r, :] = (o * jax.nn.sigmoid(mo_ref[r, :])).astype(y_ref.dtype)
        return carry

    lax.fori_loop(0, L // PREP_ROWS, fin, 0)


Z_COL_MQ = 3 * W_H // 128
Z_COL_MV = (3 * W_H + 2 * H_M * DK_M) // DV_M


def mlstm_pallas(z, zs, gate_b, norm_g, state0, B, L):
    zero_init = state0 is None
    if zero_init:
        c0 = jnp.zeros((1, 2, 1, DK_M, DV_M), F32)
        n0 = jnp.zeros((1, 2, 1, 1, DK_M), F32)
        m0 = jnp.zeros((1, 2, 1, 1, 1), F32)
        smap = lambda b, h: (0, 0, 0, 0, 0)
    else:
        c0 = state0[0]
        n0 = state0[1].reshape(B, 2, H_M, 1, DK_M)
        m0 = state0[2].reshape(B, 2, H_M, 1, 1)
        smap = lambda b, h: (b, 0, h, 0, 0)
    omap = lambda b, h: (b, 0, h, 0, 0)
    gb_row = jnp.zeros((1, 128), F32).at[0, :4 * H_M].set(gate_b.reshape(-1))
    st_specs = lambda m: [pl.BlockSpec((1, 2, 1, DK_M, DV_M), m), pl.BlockSpec((1, 2, 1, 1, DK_M), m),
                          pl.BlockSpec((1, 2, 1, 1, 1), m)]
    y, c, n, m = pl.pallas_call(
        functools.partial(_mlstm_kernel, L=L, zero_init=zero_init),
        grid=(B, H_M),
        in_specs=[pl.BlockSpec((L, DK_M), lambda b, h: (b, Z_COL_MQ + h)),
                  pl.BlockSpec((L, DK_M), lambda b, h: (b, Z_COL_MQ + H_M + h)),
                  pl.BlockSpec((L, DV_M), lambda b, h: (b, Z_COL_MV + h)),
                  pl.BlockSpec((L, DV_M), lambda b, h: (b, Z_COL_MV + H_M + h)),
                  pl.BlockSpec((L, 128), lambda b, h: (b, 0)),
                  pl.BlockSpec((1, 128), lambda b, h: (0, 0)),
                  pl.BlockSpec((1, 1, DV_M), lambda b, h: (h, 0, 0))] + st_specs(smap),
        out_specs=[pl.BlockSpec((L, DV_M), lambda b, h: (b, h))] + st_specs(omap),
        out_shape=[jax.ShapeDtypeStruct((B * L, H_M * DV_M), BF16),
                   jax.ShapeDtypeStruct((B, 2, H_M, DK_M, DV_M), F32),
                   jax.ShapeDtypeStruct((B, 2, H_M, 1, DK_M), F32),
                   jax.ShapeDtypeStruct((B, 2, H_M, 1, 1), F32)],
        scratch_shapes=[pltpu.VMEM((L, 128), F32), pltpu.VMEM((L, DV_M), F32),
                        pltpu.VMEM((2, DK_M, DV_M), F32), pltpu.VMEM((2, 8, DK_M), F32),
                        pltpu.VMEM((2, 8, 128), F32)],
        compiler_params=pltpu.CompilerParams(
            dimension_semantics=("parallel", "arbitrary"), vmem_limit_bytes=VMEM_LIMIT_BYTES),
    )(z, z, z, z, zs, gb_row, norm_g.reshape(H_M, 1, DV_M), c0, n0, m0)
    return y, (c, n.reshape(B, 2, H_M, DK_M), m.reshape(B, 2, H_M))


FBLK = 256
HY_TN = 512
HY_CONV_TN = 128


def dft_matrices(L):
    n = 2 * L
    nf = L + 1
    nfb = -(-nf // FBLK)
    nfp = nfb * FBLK
    f = jnp.arange(nfp, dtype=jnp.int32)
    t = jnp.arange(L, dtype=jnp.int32)
    ang = ((f[:, None] * t[None, :]) % n).astype(F32) * (2.0 * math.pi / n)
    valid = (f < nf)[:, None]
    c = jnp.where(valid, jnp.cos(ang), 0.0)
    s = jnp.where(valid, jnp.sin(ang), 0.0)
    fwd = jnp.concatenate([c.reshape(nfb, FBLK, L), -s.reshape(nfb, FBLK, L)], axis=1).reshape(2 * nfp, L)
    wgt = jnp.where((f == 0) | (f == L), 1.0, 2.0)[:, None] / n
    inv = jnp.concatenate([(wgt * c).reshape(nfb, FBLK, L), (-wgt * s).reshape(nfb, FBLK, L)], axis=1)
    return fwd, inv.reshape(2 * nfp, L).T


def _hy_conv_kernel(v_ref, x1_ref, x2_ref, wv_ref, w1_ref, w2_ref, v32_ref, vb_ref, x1o_ref, x2o_ref, *, L):
    def body(i, carry):
        r0 = pl.multiple_of(i * PREP_ROWS, PREP_ROWS)
        r = pl.ds(r0, PREP_ROWS)
        v = _conv3_rows(v_ref, wv_ref, r0, PREP_ROWS, L)
        v32_ref[r, :] = v
        vb_ref[r, :] = v.astype(BF16)
        x1o_ref[r, :] = _conv3_rows(x1_ref, w1_ref, r0, PREP_ROWS, L)
        x2o_ref[r, :] = _conv3_rows(x2_ref, w2_ref, r0, PREP_ROWS, L)
        return carry

    lax.fori_loop(0, L // PREP_ROWS, body, 0)


def hyena_short_conv(z, conv_w, B, L):
    nj = W_H // HY_CONV_TN
    col = lambda off: pl.BlockSpec((L, HY_CONV_TN), lambda b, j: (b, off * nj + j))
    cw = lambda off: pl.BlockSpec((3, HY_CONV_TN), lambda b, j: (0, off * nj + j))
    out = pl.BlockSpec((L, HY_CONV_TN), lambda b, j: (b, j))
    f32o = jax.ShapeDtypeStruct((B * L, W_H), F32)
    return pl.pallas_call(
        functools.partial(_hy_conv_kernel, L=L),
        grid=(B, nj),
        in_specs=[col(0), col(1), col(2), cw(0), cw(1), cw(2)],
        out_specs=[out, out, out, out],
        out_shape=[f32o, jax.ShapeDtypeStruct((B * L, W_H), BF16), f32o, f32o],
        compiler_params=pltpu.CompilerParams(
            dimension_semantics=("parallel", "parallel"), vmem_limit_bytes=VMEM_LIMIT_BYTES),
    )(z, z, z, conv_w, conv_w, conv_w)


def _dft_fwd_kernel(f_ref, u_ref, k_ref, o_ref):
    acc = jnp.dot(f_ref[...], u_ref[...], preferred_element_type=F32)
    re, im = acc[:FBLK], acc[FBLK:]
    kre, kim = k_ref[:FBLK, :], k_ref[FBLK:, :]
    o_ref[:FBLK, :] = (re * kre - im * kim).astype(o_ref.dtype)
    o_ref[FBLK:, :] = (re * kim + im * kre).astype(o_ref.dtype)


def dft_filter_fwd(fwd_b, ub, kspec, B, L):
    m2 = fwd_b.shape[0]
    nfb = m2 // (2 * FBLK)
    C = ub.shape[1]
    return pl.pallas_call(
        _dft_fwd_kernel,
        grid=(B, C // HY_TN, nfb),
        in_specs=[pl.BlockSpec((2 * FBLK, L), lambda b, j, i: (i, 0)),
                  pl.BlockSpec((L, HY_TN), lambda b, j, i: (b, j)),
                  pl.BlockSpec((2 * FBLK, HY_TN), lambda b, j, i: (i, j))],
        out_specs=pl.BlockSpec((2 * FBLK, HY_TN), lambda b, j, i: (b * nfb + i, j)),
        out_shape=jax.ShapeDtypeStruct((B * m2, C), BF16),
        compiler_params=pltpu.CompilerParams(
            dimension_semantics=("parallel", "parallel", "arbitrary"), vmem_limit_bytes=VMEM_LIMIT_BYTES),
    )(fwd_b, ub, kspec)


def _dft_inv_kernel(g_ref, s_ref, xg_ref, u_ref, b_ref, *o_refs):
    y = jnp.dot(g_ref[...], s_ref[...], preferred_element_type=F32)
    u = u_ref[...]
    out = xg_ref[...] * (y + u * b_ref[...])
    for o_ref in o_refs:
        o_ref[...] = out.astype(o_ref.dtype)


def dft_inv_gate(inv_b, spec, xg, u, bias, B, L, out_dtypes):
    m2 = inv_b.shape[1]
    C = spec.shape[1]
    tm = min(512, L)
    blk = pl.BlockSpec((tm, HY_TN), lambda b, j, i: (b * (L // tm) + i, j))
    return pl.pallas_call(
        _dft_inv_kernel,
        grid=(B, C // HY_TN, L // tm),
        in_specs=[pl.BlockSpec((tm, m2), lambda b, j, i: (i, 0)),
                  pl.BlockSpec((m2, HY_TN), lambda b, j, i: (b, j)),
                  blk, blk, pl.BlockSpec((1, HY_TN), lambda b, j, i: (0, j))],
        out_specs=[blk for _ in out_dtypes],
        out_shape=[jax.ShapeDtypeStruct((B * L, C), dt) for dt in out_dtypes],
        compiler_params=pltpu.CompilerParams(
            dimension_semantics=("parallel", "parallel", "arbitrary"), vmem_limit_bytes=VMEM_LIMIT_BYTES),
    )(inv_b, spec, xg, u, bias)


def hyena_filter_spectra(filt, fwd):
    L = filt.shape[0]
    kf = filt[:, :, 0].at[0].add(filt[0, :, 1])
    kb = filt[:, :, 1].at[0].set(0.0)
    k_all = jnp.concatenate([kf, kb], axis=1).reshape(L, 2 * HYENA_ORDER * W_H)
    f_hi, f_lo = _split2(fwd)
    k_hi, k_lo = _split2(k_all)
    mm = functools.partial(matmul, tm=2 * FBLK, tn=512)
    spec = mm(f_hi, k_lo) + mm(f_lo, k_hi) + mm(f_hi, k_hi)
    spec = spec.reshape(-1, 2 * HYENA_ORDER, W_H)
    sf, sb = spec[:, :HYENA_ORDER], spec[:, HYENA_ORDER:]
    is_im = ((jnp.arange(spec.shape[0]) // FBLK) % 2 == 1)[:, None, None]
    ks = sf + jnp.where(is_im, -sb, sb)
    return [ks[:, o] for o in range(HYENA_ORDER)]


def hyena_pallas(z, conv_w, filt, bias, mats, B, L):
    fwd, inv = mats
    fwd_b, inv_b = fwd.astype(BF16), inv.astype(BF16)
    k0, k1 = hyena_filter_spectra(filt, fwd)
    v32, vb, x1, x2 = hyena_short_conv(z, conv_w, B, L)
    b0 = bias[0].reshape(1, W_H).astype(F32)
    b1 = bias[1].reshape(1, W_H).astype(F32)
    spec = dft_filter_fwd(fwd_b, vb, k0, B, L)
    s32, sb = dft_inv_gate(inv_b, spec, x1, v32, b0, B, L, (F32, BF16))
    spec = dft_filter_fwd(fwd_b, sb, k1, B, L)
    (y,) = dft_inv_gate(inv_b, spec, x2, s32, b1, B, L, (BF16,))
    return y


def _norm_mod_kernel(x_ref, g_ref, sc_ref, sh_ref, o_ref):
    x = x_ref[...]
    y = x * lax.rsqrt(jnp.mean(x * x, axis=-1, keepdims=True) + EPS)
    o_ref[...] = ((y * g_ref[...]) * (1.0 + sc_ref[0]) + sh_ref[0]).astype(o_ref.dtype)


def norm_mod(x, g, sc, sh, L, out_dtype=BF16):
    T = x.shape[0]
    tm = 256
    per_row = sc.shape[0] > 1
    mod = pl.BlockSpec((1, 1, D_MODEL), (lambda i: (i // (L // tm), 0, 0)) if per_row else (lambda i: (0, 0, 0)))
    return pl.pallas_call(
        _norm_mod_kernel,
        grid=(T // tm,),
        in_specs=[pl.BlockSpec((tm, D_MODEL), lambda i: (i, 0)), pl.BlockSpec((1, D_MODEL), lambda i: (0, 0)),
                  mod, mod],
        out_specs=pl.BlockSpec((tm, D_MODEL), lambda i: (i, 0)),
        out_shape=jax.ShapeDtypeStruct((T, D_MODEL), out_dtype),
        compiler_params=pltpu.CompilerParams(dimension_semantics=("parallel",), vmem_limit_bytes=VMEM_LIMIT_BYTES),
    )(x, g.reshape(1, D_MODEL), sc, sh)


def _branch_mix_kernel(ya_ref, yb_ref, yc_ref, wb_ref, za_ref, zb_ref, zc_ref, o_ref):
    acc = None
    for i, (y_ref, zg_ref) in enumerate(((ya_ref, za_ref), (yb_ref, zb_ref), (yc_ref, zc_ref))):
        t = jax.nn.sigmoid(zg_ref[...]) * jnp.dot(y_ref[...], wb_ref[i], preferred_element_type=F32)
        acc = t if acc is None else acc + t
    o_ref[...] = acc.astype(o_ref.dtype)


Z_COL_ZM = (3 * W_H + 2 * H_M * DK_M + 2 * H_M * DV_M + H_G * (2 * DK_G + DV_G) + H_G * DV_G)


def branch_mix(ya, yb, yc, wb, z):
    T = ya.shape[0]
    tm, tn = 512, 512
    nj = D_MODEL // tn
    yspec = pl.BlockSpec((tm, BRANCH_W), lambda i, j: (i, 0))
    zspec = lambda br: pl.BlockSpec((tm, tn), lambda i, j: (i, Z_COL_ZM // tn + br * nj + j))
    return pl.pallas_call(
        _branch_mix_kernel,
        grid=(T // tm, nj),
        in_specs=[yspec, yspec, yspec, pl.BlockSpec((N_BRANCH, BRANCH_W, tn), lambda i, j: (0, 0, j)),
                  zspec(0), zspec(1), zspec(2)],
        out_specs=pl.BlockSpec((tm, tn), lambda i, j: (i, j)),
        out_shape=jax.ShapeDtypeStruct((T, D_MODEL), BF16),
        compiler_params=pltpu.CompilerParams(
            dimension_semantics=("parallel", "parallel"), vmem_limit_bytes=VMEM_LIMIT_BYTES),
    )(ya, yb, yc, wb, z, z, z)


def _mm_res_kernel(a_ref, w_ref, r_ref, g_ref, o_ref, *, nk):
    part = jnp.dot(a_ref[...], w_ref[...], preferred_element_type=F32)
    if nk == 1:
        o_ref[...] = r_ref[...] + g_ref[0] * part
    else:
        k = pl.program_id(2)

        @pl.when(k == 0)
        def _():
            o_ref[...] = part

        @pl.when((k != 0) & (k != nk - 1))
        def _():
            o_ref[...] += part

        @pl.when(k == nk - 1)
        def _():
            o_ref[...] = r_ref[...] + g_ref[0] * (o_ref[...] + part)


def matmul_residual(a, w, res, gate, L, *, tm, tn, tk=None):
    T, K = a.shape
    N = w.shape[1]
    tk = K if tk is None else tk
    nk = K // tk
    per_row = gate.shape[0] > 1
    tm = min(tm, L) if per_row else tm
    assert T % tm == 0 and N % tn == 0 and K % tk == 0 and (L % tm == 0 or not per_row), (T, N, K, L, tm, tn, tk)
    gspec = pl.BlockSpec((1, 1, tn), (lambda i, j, k: (i // (L // tm), 0, j)) if per_row else (lambda i, j, k: (0, 0, j)))
    return pl.pallas_call(
        functools.partial(_mm_res_kernel, nk=nk),
        grid=(T // tm, N // tn, nk),
        in_specs=[pl.BlockSpec((tm, tk), lambda i, j, k: (i, k)), pl.BlockSpec((tk, tn), lambda i, j, k: (k, j)),
                  pl.BlockSpec((tm, tn), lambda i, j, k: (i, j)), gspec],
        out_specs=pl.BlockSpec((tm, tn), lambda i, j, k: (i, j)),
        out_shape=jax.ShapeDtypeStruct((T, N), F32),
        compiler_params=pltpu.CompilerParams(
            dimension_semantics=("parallel", "parallel", "arbitrary"), vmem_limit_bytes=VMEM_LIMIT_BYTES),
    )(a, w, res, gate)


FFN_ROWS = 256
FFN_HALO = 128
FFN_CG = 256
FFN_BLOCK_ELEMS = 1 << 21


def _ffn_act_kernel(gp_ref, up_ref, w_ref, b_ref, o_ref, *, L, rows):
    width = L // rows
    wshift = width.bit_length() - 1
    n_ext = FFN_ROWS + 2 * FFN_HALO
    drs = (-1, 0, 1) if rows > 1 else (0,)
    assert rows == 1 or (width % 8 == 0 and width + 1 <= FFN_HALO)

    def body(c, carry):
        r0 = pl.multiple_of(c * FFN_ROWS, FFN_ROWS)
        p0 = pl.multiple_of(jnp.maximum(r0 - FFN_HALO, 0), FFN_HALO)
        n0 = pl.multiple_of(jnp.minimum(r0 + FFN_ROWS, L - FFN_HALO), FFN_HALO)
        r = pl.ds(r0, FFN_ROWS)
        tok = r0 + lax.broadcasted_iota(jnp.int32, (FFN_ROWS, 1), 0)
        gr = tok >> wshift
        gw = tok & (width - 1)
        ok_r = {dr: (gr + dr >= 0) & (gr + dr < rows) for dr in drs}
        ok_w = {dw: (gw + dw >= 0) & (gw + dw < width) for dw in (-1, 0, 1)}
        for cg in range(o_ref.shape[1] // FFN_CG):
            cols = slice(cg * FFN_CG, (cg + 1) * FFN_CG)
            ext = jnp.concatenate([gp_ref[pl.ds(p0, FFN_HALO), cols], gp_ref[r, cols],
                                   gp_ref[pl.ds(n0, FFN_HALO), cols]], axis=0)
            acc = jnp.zeros((FFN_ROWS, FFN_CG), F32) + b_ref[:, cols]
            for dw in (-1, 0, 1):
                e = ext if dw == 0 else pltpu.roll(ext, (-dw) % n_ext, 0)
                for dr in drs:
                    lo = FFN_HALO + dr * width
                    k = 3 * (dr + 1) + (dw + 1)
                    acc = acc + jnp.where(ok_r[dr] & ok_w[dw], e[lo:lo + FFN_ROWS], 0.0) * w_ref[k:k + 1, cols]
            o_ref[r, cols] = ((acc * jax.nn.sigmoid(acc)) * up_ref[r, cols]).astype(o_ref.dtype)
        return carry

    lax.fori_loop(0, L // FFN_ROWS, body, 0)


def ffn_act(gate_pre, up, conv_w, conv_b, B, L, rows):
    C = gate_pre.shape[1]
    ct = min(1024, max(FFN_CG, FFN_BLOCK_ELEMS // L))
    blk = pl.BlockSpec((L, ct), lambda b, j: (b, j))
    return pl.pallas_call(
        functools.partial(_ffn_act_kernel, L=L, rows=rows),
        grid=(B, C // ct),
        in_specs=[blk, blk, pl.BlockSpec((9, ct), lambda b, j: (0, j)),
                  pl.BlockSpec((1, ct), lambda b, j: (0, j))],
        out_specs=blk,
        out_shape=jax.ShapeDtypeStruct((B * L, C), BF16),
        compiler_params=pltpu.CompilerParams(
            dimension_semantics=("parallel", "parallel"), vmem_limit_bytes=VMEM_LIMIT_BYTES),
    )(gate_pre, up, conv_w.reshape(9, C), conv_b.reshape(1, C))


def hyena_filters(L, w1, b1, w2, b2, freq, w3):
    t = jnp.arange(L, dtype=F32) / L
    bands = jnp.linspace(1e-4, FILTER_BANDS - 1, FILTER_BANDS, dtype=F32)
    ang = 2.0 * math.pi * t[:, None] * bands[None, :]
    feat = jnp.concatenate([t[:, None], jnp.cos(ang), jnp.sin(ang)], axis=-1)
    fr = freq.astype(F32)
    hid = jnp.sin(fr * (feat @ w1.astype(F32) + b1.astype(F32)))
    hid = jnp.sin(fr * (hid @ w2.astype(F32) + b2.astype(F32)))
    h = (hid @ w3.astype(F32)).reshape(L, HYENA_ORDER, 2, W_H)
    deltas = jnp.abs(jnp.linspace(math.log(1e-2) / 1.5, math.log(1e-2) / 0.3, W_H, dtype=F32))
    h = h * jnp.exp(-t[:, None] * deltas[None, :])[:, None, None, :]
    return h * lax.rsqrt(jnp.sum(h * h, axis=(0, 2), keepdims=True) + EPS)


def trunk_layer(x, mod, p, B, L, rows, states):
    sh1, sc1, g1, sh2, sc2, g2 = mod
    mC, mn, mm, gS = states
    hb = norm_mod(x, p['norm_mix'], sc1, sh1, L)
    z = matmul(hb, p['w_in_main'], tm=1024, tn=512)
    zs = matmul(hb, p['w_in_small'], tm=1024, tn=N_SMALL_PAD)
    filt = hyena_filters(L, p['hyena_w1'], p['hyena_b1'], p['hyena_w2'], p['hyena_b2'], p['hyena_freq'], p['hyena_w3'])
    y_a = hyena_pallas(z, p['hyena_conv_w'], filt, p['hyena_bias'], p['dft'][L], B, L)
    y_b, m_state = mlstm_pallas(z, zs, p['mlstm_gate_b'], p['mlstm_norm'],
                                None if mC is None else (mC, mn, mm), B, L)
    y_c, g_state = gdn_pallas(z, zs, p['gdn_conv_w'], p['gdn_A_log'], p['gdn_dt_bias'], p['gdn_norm'], gS, B, L)
    mix = branch_mix(y_a, y_b, y_c, p['w_branch'], z)
    x = matmul_residual(mix, p['w_out'], x, g1, L, tm=1024, tn=512)
    hb = norm_mod(x, p['norm_ffn'], sc2, sh2, L)
    gate_pre = matmul(hb, p['ffn_w_gate'], tm=1024, tn=1024)
    up = matmul(hb, p['ffn_w_up'], tm=1024, tn=1024)
    act = ffn_act(gate_pre, up, p['ffn_conv_w'], p['ffn_conv_b'], B, L, rows)
    x = matmul_residual(act, p['ffn_w_down'], x, g2, L, tm=1024, tn=512, tk=D_FF_PAD // 2)
    return x, m_state + (g_state,)


def _ada_all(c, c_ctx, ada_w, ada_b):
    cc = jnp.concatenate([c, c_ctx[None, :], jnp.zeros((7, D_MODEL), F32)], axis=0)
    a = jax.nn.silu(cc)
    outs = []
    for l in range(DEPTH):
        outs.append(matmul(a, ada_w[l], tm=16, tn=1024) + ada_b[l])
    return outs


def kernel(x_prompt, x_sample, state_mlstm_C, state_mlstm_n, state_mlstm_m, state_gdn_S, c, c_ctx,
           ada_w, ada_b, norm_mix, norm_ffn, w_in, hyena_conv_w, hyena_w1, hyena_b1, hyena_w2, hyena_b2,
           hyena_freq, hyena_w3, hyena_bias, mlstm_gate_b, mlstm_norm, gdn_conv_w, gdn_A_log, gdn_dt_bias,
           gdn_norm, w_branch, w_out, ffn_w_gate, ffn_w_up, ffn_conv_w, ffn_conv_b, ffn_w_down, final_norm):
    bp, lp, _ = x_prompt.shape
    nb, ls, _ = x_sample.shape
    rows_lat = ls // GRID_W
    zero_states = (None, None, None, None)
    ada = _ada_all(c, c_ctx, ada_w, ada_b)
    ffpad = D_FF_PAD - D_FF
    dft = {L: dft_matrices(L) for L in sorted({lp, ls})}
    xp, xs = x_prompt.reshape(bp * lp, D_MODEL), x_sample.reshape(nb * ls, D_MODEL)
    out_C, out_n, out_m, out_S = [], [], [], []
    for l in range(DEPTH):
        wl = w_in[l]
        w_main = jnp.concatenate([wl[:, :_OFF[5]], wl[:, _OFF[6]:_OFF[8]], wl[:, _OFF[10]:]], axis=1).astype(BF16)
        w_small = jnp.concatenate([wl[:, _OFF[5]:_OFF[6]], wl[:, _OFF[8]:_OFF[10]],
                                   jnp.zeros((D_MODEL, N_SMALL_PAD - 96), F32)], axis=1).astype(BF16)
        p = {'dft': dft, 'norm_mix': norm_mix[l], 'norm_ffn': norm_ffn[l], 'w_in_main': w_main, 'w_in_small': w_small,
             'hyena_conv_w': hyena_conv_w[l], 'hyena_w1': hyena_w1[l], 'hyena_b1': hyena_b1[l],
             'hyena_w2': hyena_w2[l], 'hyena_b2': hyena_b2[l], 'hyena_freq': hyena_freq[l],
             'hyena_w3': hyena_w3[l], 'hyena_bias': hyena_bias[l],
             'mlstm_gate_b': mlstm_gate_b[l], 'mlstm_norm': mlstm_norm[l],
             'gdn_conv_w': gdn_conv_w[l], 'gdn_A_log': gdn_A_log[l], 'gdn_dt_bias': gdn_dt_bias[l],
             'gdn_norm': gdn_norm[l], 'w_branch': w_branch[l].astype(BF16), 'w_out': w_out[l].astype(BF16),
             'ffn_w_gate': jnp.pad(ffn_w_gate[l], ((0, 0), (0, ffpad))).astype(BF16),
             'ffn_w_up': jnp.pad(ffn_w_up[l], ((0, 0), (0, ffpad))).astype(BF16),
             'ffn_conv_w': jnp.pad(ffn_conv_w[l], ((0, 0), (0, 0), (0, ffpad))),
             'ffn_conv_b': jnp.pad(ffn_conv_b[l], ((0, ffpad),)),
             'ffn_w_down': jnp.pad(ffn_w_down[l], ((0, ffpad), (0, 0))).astype(BF16)}
        m = ada[l]
        mod_ctx = [t[None, None, :] for t in jnp.split(m[nb], 6, axis=-1)]
        mod_lat = [t[:, None, :] for t in jnp.split(m[:nb], 6, axis=-1)]
        xp, (sC, sn, sm, sS) = trunk_layer(xp, mod_ctx, p, bp, lp, 1, zero_states)
        out_C.append(sC)
        out_n.append(sn)
        out_m.append(sm)
        out_S.append(sS)
        cached = (state_mlstm_C[:, l].astype(F32), state_mlstm_n[:, l].astype(F32),
                  state_mlstm_m[:, l].astype(F32), state_gdn_S[:, l].astype(F32))
        xs, _ = trunk_layer(xs, mod_lat, p, nb, ls, rows_lat, cached)
    zero = jnp.zeros((1, 1, D_MODEL), F32)
    y_prompt = norm_mod(xp, final_norm, zero, zero, lp, out_dtype=F32).reshape(bp, lp, D_MODEL)
    y_sample = norm_mod(xs, final_norm, zero, zero, ls, out_dtype=F32).reshape(nb, ls, D_MODEL)
    return (y_prompt, y_sample, jnp.stack(out_C, axis=1), jnp.stack(out_n, axis=1),
            jnp.stack(out_m, axis=1), jnp.stack(out_S, axis=1))
```

```python
import functools
import math

import jax
import jax.numpy as jnp
import numpy as np
from jax import lax
from jax.experimental import pallas as pl
from jax.experimental.pallas import tpu as pltpu

D_MODEL = 4096
DEPTH = 2
GRID_W = 64
N_BRANCH = 3
BRANCH_W = D_MODEL // 2
W_H = BRANCH_W
HYENA_ORDER = 2
FILTER_BANDS = 16
H_M = 8
DV_M = BRANCH_W // H_M
DK_M = DV_M // 2
H_G = 16
DK_G = BRANCH_W // H_G
DV_G = BRANCH_W // H_G
D_FF = 11008
D_FF_PAD = 11264
CHUNK = 64
EPS = 1e-6
F32 = jnp.float32
BF16 = jnp.bfloat16

_IN_SPLITS = (3 * W_H, H_M * DK_M, H_M * DK_M, H_M * DV_M, H_M * DV_M, 4 * H_M,
              H_G * (2 * DK_G + DV_G), H_G * DV_G, 2 * H_G, 2 * H_G, N_BRANCH * D_MODEL)
_OFF = np.concatenate([[0], np.cumsum(_IN_SPLITS)]).tolist()
N_SMALL_PAD = 128

VMEM_LIMIT_BYTES = 56 * 1024 * 1024


def _mm_kernel(x_ref, w_ref, o_ref, *, nk):
    part = jnp.dot(x_ref[...].astype(BF16), w_ref[...].astype(BF16), preferred_element_type=F32)
    if nk == 1:
        o_ref[...] = part
    else:
        k = pl.program_id(2)

        @pl.when(k == 0)
        def _():
            o_ref[...] = part

        @pl.when(k != 0)
        def _():
            o_ref[...] += part


def matmul(x, w, *, tm, tn, tk=None):
    M, K = x.shape
    K2, N = w.shape
    assert K == K2
    tk = K if tk is None else tk
    tm = min(tm, M)
    assert M % tm == 0 and N % tn == 0 and K % tk == 0, (M, N, K, tm, tn, tk)
    nk = K // tk
    return pl.pallas_call(
        functools.partial(_mm_kernel, nk=nk),
        grid=(M // tm, N // tn, nk),
        in_specs=[pl.BlockSpec((tm, tk), lambda i, j, k: (i, k)),
                  pl.BlockSpec((tk, tn), lambda i, j, k: (k, j))],
        out_specs=pl.BlockSpec((tm, tn), lambda i, j, k: (i, j)),
        out_shape=jax.ShapeDtypeStruct((M, N), F32),
        compiler_params=pltpu.CompilerParams(
            dimension_semantics=("parallel", "parallel", "arbitrary"),
            vmem_limit_bytes=VMEM_LIMIT_BYTES),
    )(x, w)


def _dot(a, b):
    return jnp.dot(a.astype(BF16), b.astype(BF16), preferred_element_type=F32)


def _dot_nt(a, b):
    return lax.dot_general(a.astype(BF16), b.astype(BF16), (((1,), (1,)), ((), ())), preferred_element_type=F32)


def _dot_tn(a, b):
    return lax.dot_general(a.astype(BF16), b.astype(BF16), (((0,), (0,)), ((), ())), preferred_element_type=F32)


def _split2(a):
    hi = a.astype(BF16)
    return hi, (a - hi.astype(F32)).astype(BF16)


def _split3(a):
    hi = a.astype(BF16)
    r = a - hi.astype(F32)
    mid = r.astype(BF16)
    return hi, mid, (r - mid.astype(F32)).astype(BF16)


def _dot3(a, b):
    ah, al = _split2(a)
    bh, bl = _split2(b)
    return (jnp.dot(ah, bl, preferred_element_type=F32) + jnp.dot(al, bh, preferred_element_type=F32)
            + jnp.dot(ah, bh, preferred_element_type=F32))


def _dot_exact_lhs(a_bf16, b):
    b0, b1, b2 = _split3(b)
    return (jnp.dot(a_bf16, b2, preferred_element_type=F32) + jnp.dot(a_bf16, b1, preferred_element_type=F32)
            + jnp.dot(a_bf16, b0, preferred_element_type=F32))


def _inv_unit_triangular(ns, eye):
    xs = [eye - n for n in ns]
    ps = list(ns)
    for _ in range(5):
        ps = [_dot(p, p) for p in ps]
        xs = [x + _dot(x, p) for x, p in zip(xs, ps)]
    rs = [eye - x - _dot3(n, x) for n, x in zip(ns, xs)]
    return [x + _dot(x, r) for x, r in zip(xs, rs)]


def _softplus(x):
    return jnp.maximum(x, 0.0) + jnp.log1p(jnp.exp(-jnp.abs(x)))


def _lane_pick(x, lane):
    idx = lax.broadcasted_iota(jnp.int32, x.shape, 1)
    return jnp.sum(jnp.where(idx == lane, x, 0.0), axis=-1, keepdims=True)


def _conv3_rows(ref, w_ref, r0, nrows, total):
    x = ref[pl.ds(r0, nrows), :]
    row = lax.broadcasted_iota(jnp.int32, x.shape, 0)
    p0 = jnp.maximum(r0 - 8, 0)
    n0 = jnp.minimum(r0 + nrows, total - 8)
    prev = ref[pl.ds(pl.multiple_of(p0, 8), 8), :][7:8, :] * (r0 > 0).astype(F32)
    nxt = ref[pl.ds(pl.multiple_of(n0, 8), 8), :][0:1, :] * (r0 + nrows < total).astype(F32)
    xm1 = jnp.where(row == 0, prev, pltpu.roll(x, 1, 0))
    xp1 = jnp.where(row == nrows - 1, nxt, pltpu.roll(x, nrows - 1, 0))
    return xm1 * w_ref[0:1, :] + x * w_ref[1:2, :] + xp1 * w_ref[2:3, :]


GDN_LANE_BETA = 32
GDN_LANE_DECAY = 64
PREP_ROWS = 256
GDN_GROUP = 8


def _gdn_kernel(zq_ref, zk_ref, zv_ref, gz_ref, zs_ref, cwq_ref, cwk_ref, cwv_ref, alog_ref, dtb_ref, ng_ref,
                s0_ref, y_ref, sout_ref,
                q_s, k_s, v_s, gall_s, ball_s, o_s, wq_s, u_s, qkd_s, kd_s, eend_s, st_s, *, L, zero_init):
    h = pl.program_id(1)
    nc = L // CHUNK
    T = CHUNK
    row = lax.broadcasted_iota(jnp.int32, (T, T), 0)
    col = lax.broadcasted_iota(jnp.int32, (T, T), 1)
    eye = (row == col).astype(F32)
    ones_b = jnp.ones((T, T), BF16)
    incl = (row >= col, row <= col)
    strict = (row > col, row < col)

    @pl.when(h == 0)
    def _():
        lane = lax.broadcasted_iota(jnp.int32, (T, 128), 1)
        tril_b = incl[0].astype(BF16)
        triu_b = incl[1].astype(BF16)

        def seg(c, carry):
            r = pl.ds(pl.multiple_of(c * T, T), T)
            zs = zs_ref[r, :]
            ball_s[r, :] = jax.nn.sigmoid(zs)
            g = -jnp.exp(alog_ref[...]) * _softplus(zs + dtb_ref[...])
            pre = _dot_exact_lhs(tril_b, g)
            suf = _dot_exact_lhs(triu_b, g)
            gall_s[r, :] = jnp.where(lane < GDN_LANE_DECAY + H_G, pre, suf)
            return carry

        lax.fori_loop(0, nc, seg, 0)

    def prep(i, carry):
        r0 = pl.multiple_of(i * PREP_ROWS, PREP_ROWS)
        r = pl.ds(r0, PREP_ROWS)
        q = _conv3_rows(zq_ref, cwq_ref, r0, PREP_ROWS, L)
        q = q * jax.nn.sigmoid(q)
        q = q * lax.rsqrt(jnp.sum(q * q, axis=-1, keepdims=True) + EPS) * DK_G ** -0.5
        q_s[r, :] = q.astype(BF16)
        o_s[r, :] = jnp.zeros((PREP_ROWS, 128), F32)
        k = _conv3_rows(zk_ref, cwk_ref, r0, PREP_ROWS, L)
        k = k * jax.nn.sigmoid(k)
        k_s[r, :] = k * lax.rsqrt(jnp.sum(k * k, axis=-1, keepdims=True) + EPS)
        v = _conv3_rows(zv_ref, cwv_ref, r0, PREP_ROWS, L)
        v_s[r, :] = v * jax.nn.sigmoid(v)
        return carry

    lax.fori_loop(0, L // PREP_ROWS, prep, 0)

    grp = min(GDN_GROUP, nc)

    def phase_a(gi, carry):
        cs = [gi * grp + j for j in range(grp)]
        rs = [pl.ds(pl.multiple_of(c * T, T), T) for c in cs]
        qs = [q_s[r, :] for r in rs]
        ks = [k_s[r, :] for r in rs]
        vs = [v_s[r, :] for r in rs]
        kks = [_dot_nt(k, k) for k in ks]
        qks = [_dot_nt(q, k) for q, k in zip(qs, ks)]
        items = [(j, d) for j in range(grp) for d in range(2)]
        gcs = [_lane_pick(gall_s[rs[j], :], GDN_LANE_DECAY + d * H_G + h) for j, d in items]
        bcs = [_lane_pick(ball_s[rs[j], :], GDN_LANE_BETA + d * H_G + h) for j, d in items]
        gsq = [jnp.broadcast_to(g, (T, T)) for g in gcs]
        grs = [_dot_exact_lhs(ones_b, jnp.where(row == col, g, 0.0)) for g in gsq]
        decs = [jnp.exp(jnp.where(incl[d], g - gr, -1e30)) for (j, d), g, gr in zip(items, gsq, grs)]
        ns = [jnp.where(strict[d], b * kks[j] * dec, 0.0) for (j, d), b, dec in zip(items, bcs, decs)]
        tinvs = _inv_unit_triangular(ns, eye)
        egs = [jnp.exp(g) for g in gcs]
        ws = [_dot3(t, (b * eg) * ks[j]) for (j, d), t, b, eg in zip(items, tinvs, bcs, egs)]
        us = [_dot3(t, b * vs[j]) for (j, d), t, b in zip(items, tinvs, bcs)]
        for idx, (j, d) in enumerate(items):
            c, r, gc = cs[j], rs[j], gcs[idx]
            g_end = gc[T - 1:T, :] if d == 0 else gc[0:1, :]
            wq_s[d, pl.ds(pl.multiple_of(c * 2 * T, 2 * T), T), :] = ws[idx].astype(BF16)
            wq_s[d, pl.ds(pl.multiple_of(c * 2 * T, 2 * T) + T, T), :] = (egs[idx] * qs[j]).astype(BF16)
            u_s[d, r, :] = us[idx]
            qkd_s[d, r, :] = (qks[j] * decs[idx]).astype(BF16)
            kd_s[d, r, :] = (jnp.exp(g_end - gc) * ks[j]).astype(BF16)
            eend_s[d, pl.ds(pl.multiple_of(c * 8, 8), 8), :] = jnp.broadcast_to(jnp.exp(g_end), (8, 128))
        return carry

    lax.fori_loop(0, nc // grp, phase_a, 0)

    for d in range(2):
        if zero_init:
            st_s[d] = jnp.zeros((DK_G, DV_G), F32)
        else:
            st_s[d] = s0_ref[0, d, 0]

    def phase_b(i, carry):
        cs = (i, nc - 1 - i)
        rs = [pl.ds(pl.multiple_of(c * T, T), T) for c in cs]
        ss = [st_s[d] for d in range(2)]
        wss = [jnp.dot(wq_s[d, pl.ds(pl.multiple_of(cs[d] * 2 * T, 2 * T), 2 * T), :], ss[d].astype(BF16),
                       preferred_element_type=F32) for d in range(2)]
        vns = [(u_s[d, rs[d], :] - wss[d][:T]).astype(BF16) for d in range(2)]
        os_ = [jnp.dot(qkd_s[d, rs[d], :], vns[d], preferred_element_type=F32) for d in range(2)]
        kvs = [_dot_tn(kd_s[d, rs[d], :], vns[d]) for d in range(2)]
        for d in range(2):
            e_end = eend_s[d, pl.ds(pl.multiple_of(cs[d] * 8, 8), 8), :][0:1, :]
            st_s[d] = e_end * ss[d] + kvs[d]
        for d in range(2):
            o_s[rs[d], :] += wss[d][T:] + os_[d]
        return carry

    lax.fori_loop(0, nc, phase_b, 0)
    for d in range(2):
        sout_ref[0, d, 0] = st_s[d]

    def fin(i, carry):
        r = pl.ds(pl.multiple_of(i * PREP_ROWS, PREP_ROWS), PREP_ROWS)
        o = o_s[r, :]
        o = o * lax.rsqrt(jnp.mean(o * o, axis=-1, keepdims=True) + EPS) * ng_ref[...]
        z = gz_ref[r, :]
        y_ref[r, :] = (o * (z * jax.nn.sigmoid(z))).astype(y_ref.dtype)
        return carry

    lax.fori_loop(0, L // PREP_ROWS, fin, 0)


Z_COL_GQ = (3 * W_H + 2 * H_M * DK_M + 2 * H_M * DV_M) // 128
Z_COL_GZ = Z_COL_GQ + 3 * H_G


def gdn_pallas(z, zs, conv_w, a_log, dt_bias, norm_g, s0, B, L):
    zero_init = s0 is None
    if zero_init:
        s0 = jnp.zeros((1, 2, 1, DK_G, DV_G), F32)
        s0_map = lambda b, h: (0, 0, 0, 0, 0)
    else:
        s0_map = lambda b, h: (b, 0, h, 0, 0)
    lanes = jnp.zeros((1, 128), F32)
    alog_row = lanes.at[0, GDN_LANE_DECAY:GDN_LANE_DECAY + 2 * H_G].set(a_log.reshape(-1))
    dtb_row = lanes.at[0, GDN_LANE_DECAY:GDN_LANE_DECAY + 2 * H_G].set(dt_bias.reshape(-1))
    nc = L // CHUNK
    col = lambda off: pl.BlockSpec((L, 128), lambda b, h: (b, off + h))
    cw = lambda off: pl.BlockSpec((3, 128), lambda b, h: (0, off + h))
    one = pl.BlockSpec((1, 128), lambda b, h: (0, 0))
    st = lambda m: pl.BlockSpec((1, 2, 1, DK_G, DV_G), m)
    return pl.pallas_call(
        functools.partial(_gdn_kernel, L=L, zero_init=zero_init),
        grid=(B, H_G),
        in_specs=[col(Z_COL_GQ), col(Z_COL_GQ + H_G), col(Z_COL_GQ + 2 * H_G), col(Z_COL_GZ),
                  pl.BlockSpec((L, 128), lambda b, h: (b, 0)),
                  cw(0), cw(H_G), cw(2 * H_G), one, one, one, st(s0_map)],
        out_specs=[pl.BlockSpec((L, 128), lambda b, h: (b, h)), st(lambda b, h: (b, 0, h, 0, 0))],
        out_shape=[jax.ShapeDtypeStruct((B * L, H_G * DV_G), BF16),
                   jax.ShapeDtypeStruct((B, 2, H_G, DK_G, DV_G), F32)],
        scratch_shapes=[pltpu.VMEM((L, 128), BF16), pltpu.VMEM((L, 128), F32), pltpu.VMEM((L, 128), F32),
                        pltpu.VMEM((L, 128), F32), pltpu.VMEM((L, 128), F32), pltpu.VMEM((L, 128), F32),
                        pltpu.VMEM((2, 2 * L, 128), BF16), pltpu.VMEM((2, L, 128), F32),
                        pltpu.VMEM((2, L, CHUNK), BF16), pltpu.VMEM((2, L, 128), BF16),
                        pltpu.VMEM((2, nc * 8, 128), F32), pltpu.VMEM((2, DK_G, DV_G), F32)],
        compiler_params=pltpu.CompilerParams(
            dimension_semantics=("parallel", "arbitrary"), vmem_limit_bytes=VMEM_LIMIT_BYTES),
    )(z, z, z, z, zs, conv_w, conv_w, conv_w, alog_row, dtb_row, norm_g.reshape(1, DV_G), s0)


MLSTM_LANE_I = 0
MLSTM_LANE_F = 2 * H_M
NEG_BIG = -1e30


def _mlstm_kernel(q_ref, k_ref, v_ref, mo_ref, zs_ref, gb_ref, ng_ref, c0_ref, n0_ref, m0_ref,
                  y_ref, cout_ref, nout_ref, mout_ref, gall_s, h_s, c_s, n_s, m_s, *, L, zero_init):
    h = pl.program_id(1)
    nc = L // CHUNK
    T = CHUNK
    row = lax.broadcasted_iota(jnp.int32, (T, T), 0)
    col = lax.broadcasted_iota(jnp.int32, (T, T), 1)
    ones_b = jnp.ones((T, T), BF16)
    incl = (row >= col, row <= col)

    @pl.when(h == 0)
    def _():
        lane = lax.broadcasted_iota(jnp.int32, (T, 128), 1)
        tril_b = incl[0].astype(BF16)
        triu_b = incl[1].astype(BF16)

        def seg(c, carry):
            r = pl.ds(pl.multiple_of(c * T, T), T)
            gp = zs_ref[r, :] + gb_ref[...]
            lf = jnp.minimum(gp, 0.0) - jnp.log1p(jnp.exp(-jnp.abs(gp)))
            pre = _dot_exact_lhs(tril_b, lf)
            suf = _dot_exact_lhs(triu_b, lf)
            gall_s[r, :] = jnp.where(lane < MLSTM_LANE_F, gp, jnp.where(lane < MLSTM_LANE_F + H_M, pre, suf))
            return carry

        lax.fori_loop(0, nc, seg, 0)

    for d in range(2):
        if zero_init:
            c_s[d] = jnp.zeros((DK_M, DV_M), F32)
            n_s[d] = jnp.zeros((8, DK_M), F32)
            m_s[d] = jnp.zeros((8, 128), F32)
        else:
            c_s[d] = c0_ref[0, d, 0]
            n_s[d] = jnp.broadcast_to(n0_ref[0, d, 0], (8, DK_M))
            m_s[d] = jnp.broadcast_to(m0_ref[0, d, 0], (8, 128))

    def zero_h(i, carry):
        h_s[pl.ds(pl.multiple_of(i * PREP_ROWS, PREP_ROWS), PREP_ROWS), :] = jnp.zeros((PREP_ROWS, DV_M), F32)
        return carry

    lax.fori_loop(0, L // PREP_ROWS, zero_h, 0)

    def step(i, carry):
        D = range(2)
        cs = (i, nc - 1 - i)
        rs = [pl.ds(pl.multiple_of(c * T, T), T) for c in cs]
        qs = [q_ref[r, :] for r in rs]
        ks = [k_ref[r, :] * DK_M ** -0.5 for r in rs]
        vs = [v_ref[r, :].astype(BF16) for r in rs]
        qbs = [q.astype(BF16) for q in qs]
        qks = [_dot_nt(qbs[d], ks[d]) for d in D]
        cst = [c_s[d] for d in D]
        qcs = [jnp.dot(qbs[d], cst[d].astype(BF16), preferred_element_type=F32) for d in D]
        bcol = [_lane_pick(gall_s[rs[d], :], MLSTM_LANE_F + d * H_M + h) for d in D]
        icol = [_lane_pick(gall_s[rs[d], :], MLSTM_LANE_I + d * H_M + h) for d in D]
        amb = [icol[d] - bcol[d] for d in D]
        rmat = [_dot_exact_lhs(ones_b, jnp.where(row == col, jnp.broadcast_to(amb[d], (T, T)), 0.0)) for d in D]
        dmat = [jnp.where(incl[d], bcol[d] + rmat[d], NEG_BIG) for d in D]
        m = [m_s[d][0:1, 0:1] for d in D]
        inter = [bcol[d] + m[d] for d in D]
        m_t = [jnp.maximum(inter[d], jnp.max(dmat[d], axis=-1, keepdims=True)) for d in D]
        smat = [qks[d] * jnp.exp(dmat[d] - m_t[d]) for d in D]
        w_inter = [jnp.exp(inter[d] - m_t[d]) for d in D]
        sv = [jnp.dot(smat[d].astype(BF16), vs[d], preferred_element_type=F32) for d in D]
        b_end = [bcol[0][T - 1:T, :], bcol[1][0:1, :]]
        gcol = [b_end[d] + amb[d] for d in D]
        m_new = [jnp.maximum(b_end[d] + m[d], jnp.max(gcol[d], axis=0, keepdims=True)) for d in D]
        kw = [jnp.exp(gcol[d] - m_new[d]) * ks[d] for d in D]
        kv = [_dot_tn(kw[d], vs[d]) for d in D]
        for d in D:
            n = n_s[d][0:1, :]
            cd = jnp.exp(b_end[d] + m[d] - m_new[d])
            num = sv[d] + w_inter[d] * qcs[d]
            den = (jnp.sum(smat[d], axis=-1, keepdims=True)
                   + w_inter[d] * jnp.sum(qs[d] * n, axis=-1, keepdims=True))
            h_s[rs[d], :] += num / jnp.maximum(jnp.abs(den), jnp.exp(-m_t[d]))
            c_s[d] = cd * cst[d] + kv[d]
            n_s[d] = jnp.broadcast_to(cd * n + jnp.sum(kw[d], axis=0, keepdims=True), (8, DK_M))
            m_s[d] = jnp.broadcast_to(m_new[d], (8, 128))
        return carry

    lax.fori_loop(0, nc, step, 0)
    for d in range(2):
        cout_ref[0, d, 0] = c_s[d]
        nout_ref[0, d, 0] = n_s[d][0:1, :]
        mout_ref[0, d, 0] = m_s[d][0:1, 0:1]

    def fin(i, carry):
        r = pl.ds(pl.multiple_of(i * PREP_ROWS, PREP_ROWS), PREP_ROWS)
        o = h_s[r, :]
        o = o * lax.rsqrt(jnp.mean(o * o, axis=-1, keepdims=True) + EPS) * ng_ref[0]
        y_ref[r, :] = (o * jax.nn.sigmoid(mo_ref[r, :])).astype(y_ref.dtype)
        return carry

    lax.fori_loop(0, L // PREP_ROWS, fin, 0)


Z_COL_MQ = 3 * W_H // 128
Z_COL_MV = (3 * W_H + 2 * H_M * DK_M) // DV_M


def mlstm_pallas(z, zs, gate_b, norm_g, state0, B, L):
    zero_init = state0 is None
    if zero_init:
        c0 = jnp.zeros((1, 2, 1, DK_M, DV_M), F32)
        n0 = jnp.zeros((1, 2, 1, 1, DK_M), F32)
        m0 = jnp.zeros((1, 2, 1, 1, 1), F32)
        smap = lambda b, h: (0, 0, 0, 0, 0)
    else:
        c0 = state0[0]
        n0 = state0[1].reshape(B, 2, H_M, 1, DK_M)
        m0 = state0[2].reshape(B, 2, H_M, 1, 1)
        smap = lambda b, h: (b, 0, h, 0, 0)
    omap = lambda b, h: (b, 0, h, 0, 0)
    gb_row = jnp.zeros((1, 128), F32).at[0, :4 * H_M].set(gate_b.reshape(-1))
    st_specs = lambda m: [pl.BlockSpec((1, 2, 1, DK_M, DV_M), m), pl.BlockSpec((1, 2, 1, 1, DK_M), m),
                          pl.BlockSpec((1, 2, 1, 1, 1), m)]
    y, c, n, m = pl.pallas_call(
        functools.partial(_mlstm_kernel, L=L, zero_init=zero_init),
        grid=(B, H_M),
        in_specs=[pl.BlockSpec((L, DK_M), lambda b, h: (b, Z_COL_MQ + h)),
                  pl.BlockSpec((L, DK_M), lambda b, h: (b, Z_COL_MQ + H_M + h)),
                  pl.BlockSpec((L, DV_M), lambda b, h: (b, Z_COL_MV + h)),
                  pl.BlockSpec((L, DV_M), lambda b, h: (b, Z_COL_MV + H_M + h)),
                  pl.BlockSpec((L, 128), lambda b, h: (b, 0)),
                  pl.BlockSpec((1, 128), lambda b, h: (0, 0)),
                  pl.BlockSpec((1, 1, DV_M), lambda b, h: (h, 0, 0))] + st_specs(smap),
        out_specs=[pl.BlockSpec((L, DV_M), lambda b, h: (b, h))] + st_specs(omap),
        out_shape=[jax.ShapeDtypeStruct((B * L, H_M * DV_M), BF16),
                   jax.ShapeDtypeStruct((B, 2, H_M, DK_M, DV_M), F32),
                   jax.ShapeDtypeStruct((B, 2, H_M, 1, DK_M), F32),
                   jax.ShapeDtypeStruct((B, 2, H_M, 1, 1), F32)],
        scratch_shapes=[pltpu.VMEM((L, 128), F32), pltpu.VMEM((L, DV_M), F32),
                        pltpu.VMEM((2, DK_M, DV_M), F32), pltpu.VMEM((2, 8, DK_M), F32),
                        pltpu.VMEM((2, 8, 128), F32)],
        compiler_params=pltpu.CompilerParams(
            dimension_semantics=("parallel", "arbitrary"), vmem_limit_bytes=VMEM_LIMIT_BYTES),
    )(z, z, z, z, zs, gb_row, norm_g.reshape(H_M, 1, DV_M), c0, n0, m0)
    return y, (c, n.reshape(B, 2, H_M, DK_M), m.reshape(B, 2, H_M))


FBLK = 256
HY_TN = 512
HY_CONV_TN = 128


def dft_matrices(L):
    n = 2 * L
    nf = L + 1
    nfb = -(-nf // FBLK)
    nfp = nfb * FBLK
    f = jnp.arange(nfp, dtype=jnp.int32)
    t = jnp.arange(L, dtype=jnp.int32)
    ang = ((f[:, None] * t[None, :]) % n).astype(F32) * (2.0 * math.pi / n)
    valid = (f < nf)[:, None]
    c = jnp.where(valid, jnp.cos(ang), 0.0)
    s = jnp.where(valid, jnp.sin(ang), 0.0)
    fwd = jnp.concatenate([c.reshape(nfb, FBLK, L), -s.reshape(nfb, FBLK, L)], axis=1).reshape(2 * nfp, L)
    wgt = jnp.where((f == 0) | (f == L), 1.0, 2.0)[:, None] / n
    inv = jnp.concatenate([(wgt * c).reshape(nfb, FBLK, L), (-wgt * s).reshape(nfb, FBLK, L)], axis=1)
    return fwd, inv.reshape(2 * nfp, L).T


def _hy_conv_kernel(v_ref, x1_ref, x2_ref, wv_ref, w1_ref, w2_ref, v32_ref, vb_ref, x1o_ref, x2o_ref, *, L):
    def body(i, carry):
        r0 = pl.multiple_of(i * PREP_ROWS, PREP_ROWS)
        r = pl.ds(r0, PREP_ROWS)
        v = _conv3_rows(v_ref, wv_ref, r0, PREP_ROWS, L)
        v32_ref[r, :] = v
        vb_ref[r, :] = v.astype(BF16)
        x1o_ref[r, :] = _conv3_rows(x1_ref, w1_ref, r0, PREP_ROWS, L)
        x2o_ref[r, :] = _conv3_rows(x2_ref, w2_ref, r0, PREP_ROWS, L)
        return carry

    lax.fori_loop(0, L // PREP_ROWS, body, 0)


def hyena_short_conv(z, conv_w, B, L):
    nj = W_H // HY_CONV_TN
    col = lambda off: pl.BlockSpec((L, HY_CONV_TN), lambda b, j: (b, off * nj + j))
    cw = lambda off: pl.BlockSpec((3, HY_CONV_TN), lambda b, j: (0, off * nj + j))
    out = pl.BlockSpec((L, HY_CONV_TN), lambda b, j: (b, j))
    f32o = jax.ShapeDtypeStruct((B * L, W_H), F32)
    return pl.pallas_call(
        functools.partial(_hy_conv_kernel, L=L),
        grid=(B, nj),
        in_specs=[col(0), col(1), col(2), cw(0), cw(1), cw(2)],
        out_specs=[out, out, out, out],
        out_shape=[f32o, jax.ShapeDtypeStruct((B * L, W_H), BF16), f32o, f32o],
        compiler_params=pltpu.CompilerParams(
            dimension_semantics=("parallel", "parallel"), vmem_limit_bytes=VMEM_LIMIT_BYTES),
    )(z, z, z, conv_w, conv_w, conv_w)


def _dft_fwd_kernel(f_ref, u_ref, k_ref, o_ref):
    acc = jnp.dot(f_ref[...], u_ref[...], preferred_element_type=F32)
    re, im = acc[:FBLK], acc[FBLK:]
    kre, kim = k_ref[:FBLK, :], k_ref[FBLK:, :]
    o_ref[:FBLK, :] = (re * kre - im * kim).astype(o_ref.dtype)
    o_ref[FBLK:, :] = (re * kim + im * kre).astype(o_ref.dtype)


def dft_filter_fwd(fwd_b, ub, kspec, B, L):
    m2 = fwd_b.shape[0]
    nfb = m2 // (2 * FBLK)
    C = ub.shape[1]
    return pl.pallas_call(
        _dft_fwd_kernel,
        grid=(B, C // HY_TN, nfb),
        in_specs=[pl.BlockSpec((2 * FBLK, L), lambda b, j, i: (i, 0)),
                  pl.BlockSpec((L, HY_TN), lambda b, j, i: (b, j)),
                  pl.BlockSpec((2 * FBLK, HY_TN), lambda b, j, i: (i, j))],
        out_specs=pl.BlockSpec((2 * FBLK, HY_TN), lambda b, j, i: (b * nfb + i, j)),
        out_shape=jax.ShapeDtypeStruct((B * m2, C), BF16),
        compiler_params=pltpu.CompilerParams(
            dimension_semantics=("parallel", "parallel", "arbitrary"), vmem_limit_bytes=VMEM_LIMIT_BYTES),
    )(fwd_b, ub, kspec)


def _dft_inv_kernel(g_ref, s_ref, xg_ref, u_ref, b_ref, *o_refs):
    y = jnp.dot(g_ref[...], s_ref[...], preferred_element_type=F32)
    u = u_ref[...]
    out = xg_ref[...] * (y + u * b_ref[...])
    for o_ref in o_refs:
        o_ref[...] = out.astype(o_ref.dtype)


def dft_inv_gate(inv_b, spec, xg, u, bias, B, L, out_dtypes):
    m2 = inv_b.shape[1]
    C = spec.shape[1]
    tm = min(512, L)
    blk = pl.BlockSpec((tm, HY_TN), lambda b, j, i: (b * (L // tm) + i, j))
    return pl.pallas_call(
        _dft_inv_kernel,
        grid=(B, C // HY_TN, L // tm),
        in_specs=[pl.BlockSpec((tm, m2), lambda b, j, i: (i, 0)),
                  pl.BlockSpec((m2, HY_TN), lambda b, j, i: (b, j)),
                  blk, blk, pl.BlockSpec((1, HY_TN), lambda b, j, i: (0, j))],
        out_specs=[blk for _ in out_dtypes],
        out_shape=[jax.ShapeDtypeStruct((B * L, C), dt) for dt in out_dtypes],
        compiler_params=pltpu.CompilerParams(
            dimension_semantics=("parallel", "parallel", "arbitrary"), vmem_limit_bytes=VMEM_LIMIT_BYTES),
    )(inv_b, spec, xg, u, bias)


def hyena_filter_spectra(filt, fwd):
    L = filt.shape[0]
    kf = filt[:, :, 0].at[0].add(filt[0, :, 1])
    kb = filt[:, :, 1].at[0].set(0.0)
    k_all = jnp.concatenate([kf, kb], axis=1).reshape(L, 2 * HYENA_ORDER * W_H)
    f_hi, f_lo = _split2(fwd)
    k_hi, k_lo = _split2(k_all)
    mm = functools.partial(matmul, tm=2 * FBLK, tn=512)
    spec = mm(f_hi, k_lo) + mm(f_lo, k_hi) + mm(f_hi, k_hi)
    spec = spec.reshape(-1, 2 * HYENA_ORDER, W_H)
    sf, sb = spec[:, :HYENA_ORDER], spec[:, HYENA_ORDER:]
    is_im = ((jnp.arange(spec.shape[0]) // FBLK) % 2 == 1)[:, None, None]
    ks = sf + jnp.where(is_im, -sb, sb)
    return [ks[:, o] for o in range(HYENA_ORDER)]


def hyena_pallas(z, conv_w, filt, bias, mats, B, L):
    fwd, inv = mats
    fwd_b, inv_b = fwd.astype(BF16), inv.astype(BF16)
    k0, k1 = hyena_filter_spectra(filt, fwd)
    v32, vb, x1, x2 = hyena_short_conv(z, conv_w, B, L)
    b0 = bias[0].reshape(1, W_H).astype(F32)
    b1 = bias[1].reshape(1, W_H).astype(F32)
    spec = dft_filter_fwd(fwd_b, vb, k0, B, L)
    s32, sb = dft_inv_gate(inv_b, spec, x1, v32, b0, B, L, (F32, BF16))
    spec = dft_filter_fwd(fwd_b, sb, k1, B, L)
    (y,) = dft_inv_gate(inv_b, spec, x2, s32, b1, B, L, (BF16,))
    return y


FFT_N2 = 128
FFT_TN = 16384
FFT_MIN_L = 8 * FFT_N2


def fft_matrices(L):
    n = 2 * L
    n1 = n // FFT_N2
    f1 = jnp.arange(n1, dtype=jnp.int32)
    t1 = jnp.arange(n1 // 2, dtype=jnp.int32)
    t2 = jnp.arange(FFT_N2, dtype=jnp.int32)
    ang_a = ((f1[:, None] * t1[None, :]) % n1).astype(F32) * (2.0 * math.pi / n1)
    ca, sa = jnp.cos(ang_a), jnp.sin(ang_a)
    m_a = jnp.stack([ca, -sa], axis=1).reshape(2 * n1, n1 // 2)
    m_c = m_a.T / n
    ang_t = (f1[:, None] * t2[None, :]).astype(F32) * (2.0 * math.pi / n)
    twc = jnp.cos(ang_t).reshape(n1 * FFT_N2, 1)
    tws = jnp.sin(ang_t).reshape(n1 * FFT_N2, 1)
    ang_2 = ((t2[:, None] * t2[None, :]) % FFT_N2).astype(F32) * (2.0 * math.pi / FFT_N2)
    c2, s2 = jnp.cos(ang_2), jnp.sin(ang_2)
    m_f = jnp.concatenate([jnp.concatenate([c2, s2], axis=1), jnp.concatenate([-s2, c2], axis=1)], axis=0)
    m_i = jnp.concatenate([jnp.concatenate([c2, -s2], axis=1), jnp.concatenate([s2, c2], axis=1)], axis=0)
    return dict(n1=n1, m_a=m_a, m_c=m_c, twc=twc, tws=tws, m_f=m_f, m_i=m_i)


def _mm_any(a, b, precise):
    if precise:
        return _dot3(a, b)
    return jnp.dot(a.astype(BF16), b.astype(BF16), preferred_element_type=F32)


def _fft_a_kernel(m_ref, u_ref, o_ref, *, precise):
    o_ref[...] = _mm_any(m_ref[...], u_ref[...], precise)


def fft_stage_a(m_a, u, nb, L, precise=False):
    C = u.shape[1]
    n1 = m_a.shape[0] // 2
    ncol = FFT_N2 * C
    uv = u.reshape(nb * (n1 // 2), ncol)
    out = pl.pallas_call(
        functools.partial(_fft_a_kernel, precise=precise),
        grid=(nb, ncol // FFT_TN),
        in_specs=[pl.BlockSpec((2 * n1, n1 // 2), lambda b, j: (0, 0)),
                  pl.BlockSpec((n1 // 2, FFT_TN), lambda b, j: (b, j))],
        out_specs=pl.BlockSpec((2 * n1, FFT_TN), lambda b, j: (b, j)),
        out_shape=jax.ShapeDtypeStruct((nb * 2 * n1, ncol), F32),
        compiler_params=pltpu.CompilerParams(
            dimension_semantics=("parallel", "parallel"), vmem_limit_bytes=VMEM_LIMIT_BYTES),
    )(m_a if precise else m_a.astype(BF16), uv)
    return out.reshape(nb * 2 * n1 * FFT_N2, C)


def _fft_mid_kernel(a_ref, twc_ref, tws_ref, mf_ref, mi_ref, k_ref, o_ref, *, inverse, precise):
    n2 = FFT_N2
    c, s = twc_ref[...], tws_ref[...]
    are, aim = a_ref[:n2, :], a_ref[n2:, :]
    a = jnp.concatenate([are * c + aim * s, aim * c - are * s], axis=0)
    x = _mm_any(mf_ref[...], a, precise)
    if not inverse:
        o_ref[...] = x
        return
    xre, xim = x[:n2], x[n2:]
    kre, kim = k_ref[:n2, :], k_ref[n2:, :]
    y = jnp.concatenate([xre * kre - xim * kim, xre * kim + xim * kre], axis=0)
    bp = _mm_any(mi_ref[...], y, precise)
    bre, bim = bp[:n2], bp[n2:]
    o_ref[:n2, :] = (bre * c - bim * s).astype(o_ref.dtype)
    o_ref[n2:, :] = (bre * s + bim * c).astype(o_ref.dtype)


def fft_stage_mid(a, mats, kspec, nb, inverse=True, precise=False):
    C = a.shape[1]
    n1 = mats['n1']
    slab = pl.BlockSpec((2 * FFT_N2, C), lambda b, f: (b * n1 + f, 0))
    tw = pl.BlockSpec((FFT_N2, 1), lambda b, f: (f, 0))
    mat = pl.BlockSpec((2 * FFT_N2, 2 * FFT_N2), lambda b, f: (0, 0))
    cast = (lambda m: m) if precise else (lambda m: m.astype(BF16))
    if kspec is None:
        kspec = jnp.zeros((2 * FFT_N2, C), F32)
        kmap = lambda b, f: (0, 0)
    else:
        kmap = lambda b, f: (f, 0)
    return pl.pallas_call(
        functools.partial(_fft_mid_kernel, inverse=inverse, precise=precise),
        grid=(nb, n1),
        in_specs=[slab, tw, tw, mat, mat, pl.BlockSpec((2 * FFT_N2, C), kmap)],
        out_specs=slab,
        out_shape=jax.ShapeDtypeStruct(a.shape, BF16 if inverse else F32),
        compiler_params=pltpu.CompilerParams(
            dimension_semantics=("parallel", "parallel"), vmem_limit_bytes=VMEM_LIMIT_BYTES),
    )(a, mats['twc'], mats['tws'], cast(mats['m_f']), cast(mats['m_i']), kspec)


def _fft_c_kernel(m_ref, b_ref, xg_ref, u_ref, bias_ref, *o_refs):
    y = jnp.dot(m_ref[...], b_ref[...], preferred_element_type=F32)
    out = xg_ref[...] * (y + u_ref[...] * bias_ref[...])
    for o_ref in o_refs:
        o_ref[...] = out.astype(o_ref.dtype)


def fft_stage_c(m_c, bsp, xg, u, bias, nb, L, out_dtypes):
    C = xg.shape[1]
    n1 = m_c.shape[1] // 2
    ncol = FFT_N2 * C
    blk = pl.BlockSpec((n1 // 2, FFT_TN), lambda b, j: (b, j))
    outs = pl.pallas_call(
        _fft_c_kernel,
        grid=(nb, ncol // FFT_TN),
        in_specs=[pl.BlockSpec((n1 // 2, 2 * n1), lambda b, j: (0, 0)),
                  pl.BlockSpec((2 * n1, FFT_TN), lambda b, j: (b, j)),
                  blk, blk, pl.BlockSpec((1, FFT_TN), lambda b, j: (0, j))],
        out_specs=[blk for _ in out_dtypes],
        out_shape=[jax.ShapeDtypeStruct((nb * (n1 // 2), ncol), dt) for dt in out_dtypes],
        compiler_params=pltpu.CompilerParams(
            dimension_semantics=("parallel", "parallel"), vmem_limit_bytes=VMEM_LIMIT_BYTES),
    )(m_c.astype(BF16), bsp.reshape(nb * 2 * n1, ncol), xg.reshape(nb * (n1 // 2), ncol),
      u.reshape(nb * (n1 // 2), ncol), jnp.tile(bias, (1, FFT_N2)))
    return [o.reshape(nb * L, C) for o in outs]


def fft_filter_spectra(filt, mats):
    L = filt.shape[0]
    kf = filt[:, :, 0].at[0].add(filt[0, :, 1])
    kb = filt[:, :, 1].at[0].set(0.0)
    nsig = 2 * HYENA_ORDER
    k_all = jnp.moveaxis(jnp.concatenate([kf, kb], axis=1), 1, 0).reshape(nsig * L, W_H)
    a = fft_stage_a(mats['m_a'], k_all, nsig, L, precise=True)
    spec = fft_stage_mid(a, mats, None, nsig, inverse=False, precise=True).reshape(nsig, -1, W_H)
    is_im = ((jnp.arange(spec.shape[1]) // FFT_N2) % 2 == 1)[None, :, None]
    sf, sb = spec[:HYENA_ORDER], spec[HYENA_ORDER:]
    ks = sf + jnp.where(is_im, -sb, sb)
    return [ks[o] for o in range(HYENA_ORDER)]


def hyena_fft_pallas(z, conv_w, filt, bias, mats, B, L):
    k0, k1 = fft_filter_spectra(filt, mats)
    v32, _, x1, x2 = hyena_short_conv(z, conv_w, B, L)
    b0 = bias[0].reshape(1, W_H).astype(F32)
    b1 = bias[1].reshape(1, W_H).astype(F32)
    a = fft_stage_a(mats['m_a'], v32, B, L)
    bs = fft_stage_mid(a, mats, k0, B)
    (s32,) = fft_stage_c(mats['m_c'], bs, x1, v32, b0, B, L, (F32,))
    a = fft_stage_a(mats['m_a'], s32, B, L)
    bs = fft_stage_mid(a, mats, k1, B)
    (y,) = fft_stage_c(mats['m_c'], bs, x2, s32, b1, B, L, (BF16,))
    return y


def _norm_mod_kernel(x_ref, g_ref, sc_ref, sh_ref, o_ref):
    x = x_ref[...]
    y = x * lax.rsqrt(jnp.mean(x * x, axis=-1, keepdims=True) + EPS)
    o_ref[...] = ((y * g_ref[...]) * (1.0 + sc_ref[0]) + sh_ref[0]).astype(o_ref.dtype)


def norm_mod(x, g, sc, sh, L, out_dtype=BF16):
    T = x.shape[0]
    tm = 256
    per_row = sc.shape[0] > 1
    mod = pl.BlockSpec((1, 1, D_MODEL), (lambda i: (i // (L // tm), 0, 0)) if per_row else (lambda i: (0, 0, 0)))
    return pl.pallas_call(
        _norm_mod_kernel,
        grid=(T // tm,),
        in_specs=[pl.BlockSpec((tm, D_MODEL), lambda i: (i, 0)), pl.BlockSpec((1, D_MODEL), lambda i: (0, 0)),
                  mod, mod],
        out_specs=pl.BlockSpec((tm, D_MODEL), lambda i: (i, 0)),
        out_shape=jax.ShapeDtypeStruct((T, D_MODEL), out_dtype),
        compiler_params=pltpu.CompilerParams(dimension_semantics=("parallel",), vmem_limit_bytes=VMEM_LIMIT_BYTES),
    )(x, g.reshape(1, D_MODEL), sc, sh)


def _branch_mix_kernel(ya_ref, yb_ref, yc_ref, wb_ref, za_ref, zb_ref, zc_ref, o_ref):
    acc = None
    for i, (y_ref, zg_ref) in enumerate(((ya_ref, za_ref), (yb_ref, zb_ref), (yc_ref, zc_ref))):
        t = jax.nn.sigmoid(zg_ref[...]) * jnp.dot(y_ref[...], wb_ref[i], preferred_element_type=F32)
        acc = t if acc is None else acc + t
    o_ref[...] = acc.astype(o_ref.dtype)


Z_COL_ZM = (3 * W_H + 2 * H_M * DK_M + 2 * H_M * DV_M + H_G * (2 * DK_G + DV_G) + H_G * DV_G)


def branch_mix(ya, yb, yc, wb, z):
    T = ya.shape[0]
    tm, tn = 512, 512
    nj = D_MODEL // tn
    yspec = pl.BlockSpec((tm, BRANCH_W), lambda i, j: (i, 0))
    zspec = lambda br: pl.BlockSpec((tm, tn), lambda i, j: (i, Z_COL_ZM // tn + br * nj + j))
    return pl.pallas_call(
        _branch_mix_kernel,
        grid=(T // tm, nj),
        in_specs=[yspec, yspec, yspec, pl.BlockSpec((N_BRANCH, BRANCH_W, tn), lambda i, j: (0, 0, j)),
                  zspec(0), zspec(1), zspec(2)],
        out_specs=pl.BlockSpec((tm, tn), lambda i, j: (i, j)),
        out_shape=jax.ShapeDtypeStruct((T, D_MODEL), BF16),
        compiler_params=pltpu.CompilerParams(
            dimension_semantics=("parallel", "parallel"), vmem_limit_bytes=VMEM_LIMIT_BYTES),
    )(ya, yb, yc, wb, z, z, z)


def _mm_res_kernel(a_ref, w_ref, r_ref, g_ref, o_ref, *, nk):
    part = jnp.dot(a_ref[...], w_ref[...], preferred_element_type=F32)
    if nk == 1:
        o_ref[...] = r_ref[...] + g_ref[0] * part
    else:
        k = pl.program_id(2)

        @pl.when(k == 0)
        def _():
            o_ref[...] = part

        @pl.when((k != 0) & (k != nk - 1))
        def _():
            o_ref[...] += part

        @pl.when(k == nk - 1)
        def _():
            o_ref[...] = r_ref[...] + g_ref[0] * (o_ref[...] + part)


def matmul_residual(a, w, res, gate, L, *, tm, tn, tk=None):
    T, K = a.shape
    N = w.shape[1]
    tk = K if tk is None else tk
    nk = K // tk
    per_row = gate.shape[0] > 1
    tm = min(tm, L) if per_row else tm
    assert T % tm == 0 and N % tn == 0 and K % tk == 0 and (L % tm == 0 or not per_row), (T, N, K, L, tm, tn, tk)
    gspec = pl.BlockSpec((1, 1, tn), (lambda i, j, k: (i // (L // tm), 0, j)) if per_row else (lambda i, j, k: (0, 0, j)))
    return pl.pallas_call(
        functools.partial(_mm_res_kernel, nk=nk),
        grid=(T // tm, N // tn, nk),
        in_specs=[pl.BlockSpec((tm, tk), lambda i, j, k: (i, k)), pl.BlockSpec((tk, tn), lambda i, j, k: (k, j)),
                  pl.BlockSpec((tm, tn), lambda i, j, k: (i, j)), gspec],
        out_specs=pl.BlockSpec((tm, tn), lambda i, j, k: (i, j)),
        out_shape=jax.ShapeDtypeStruct((T, N), F32),
        compiler_params=pltpu.CompilerParams(
            dimension_semantics=("parallel", "parallel", "arbitrary"), vmem_limit_bytes=VMEM_LIMIT_BYTES),
    )(a, w, res, gate)


FFN_ROWS = 256
FFN_HALO = 128
FFN_CG = 256
FFN_BLOCK_ELEMS = 1 << 21


def _ffn_act_kernel(gp_ref, up_ref, w_ref, b_ref, o_ref, *, L, rows):
    width = L // rows
    wshift = width.bit_length() - 1
    n_ext = FFN_ROWS + 2 * FFN_HALO
    drs = (-1, 0, 1) if rows > 1 else (0,)
    assert rows == 1 or (width % 8 == 0 and width + 1 <= FFN_HALO)

    def body(c, carry):
        r0 = pl.multiple_of(c * FFN_ROWS, FFN_ROWS)
        p0 = pl.multiple_of(jnp.maximum(r0 - FFN_HALO, 0), FFN_HALO)
        n0 = pl.multiple_of(jnp.minimum(r0 + FFN_ROWS, L - FFN_HALO), FFN_HALO)
        r = pl.ds(r0, FFN_ROWS)
        tok = r0 + lax.broadcasted_iota(jnp.int32, (FFN_ROWS, 1), 0)
        gr = tok >> wshift
        gw = tok & (width - 1)
        ok_r = {dr: (gr + dr >= 0) & (gr + dr < rows) for dr in drs}
        ok_w = {dw: (gw + dw >= 0) & (gw + dw < width) for dw in (-1, 0, 1)}
        for cg in range(o_ref.shape[1] // FFN_CG):
            cols = slice(cg * FFN_CG, (cg + 1) * FFN_CG)
            ext = jnp.concatenate([gp_ref[pl.ds(p0, FFN_HALO), cols], gp_ref[r, cols],
                                   gp_ref[pl.ds(n0, FFN_HALO), cols]], axis=0)
            acc = jnp.zeros((FFN_ROWS, FFN_CG), F32) + b_ref[:, cols]
            for dw in (-1, 0, 1):
                e = ext if dw == 0 else pltpu.roll(ext, (-dw) % n_ext, 0)
                for dr in drs:
                    lo = FFN_HALO + dr * width
                    k = 3 * (dr + 1) + (dw + 1)
                    acc = acc + jnp.where(ok_r[dr] & ok_w[dw], e[lo:lo + FFN_ROWS], 0.0) * w_ref[k:k + 1, cols]
            o_ref[r, cols] = ((acc * jax.nn.sigmoid(acc)) * up_ref[r, cols]).astype(o_ref.dtype)
        return carry

    lax.fori_loop(0, L // FFN_ROWS, body, 0)


def ffn_act(gate_pre, up, conv_w, conv_b, B, L, rows):
    C = gate_pre.shape[1]
    ct = min(1024, max(FFN_CG, FFN_BLOCK_ELEMS // L))
    blk = pl.BlockSpec((L, ct), lambda b, j: (b, j))
    return pl.pallas_call(
        functools.partial(_ffn_act_kernel, L=L, rows=rows),
        grid=(B, C // ct),
        in_specs=[blk, blk, pl.BlockSpec((9, ct), lambda b, j: (0, j)),
                  pl.BlockSpec((1, ct), lambda b, j: (0, j))],
        out_specs=blk,
        out_shape=jax.ShapeDtypeStruct((B * L, C), BF16),
        compiler_params=pltpu.CompilerParams(
            dimension_semantics=("parallel", "parallel"), vmem_limit_bytes=VMEM_LIMIT_BYTES),
    )(gate_pre, up, conv_w.reshape(9, C), conv_b.reshape(1, C))


def hyena_filters(L, w1, b1, w2, b2, freq, w3):
    t = jnp.arange(L, dtype=F32) / L
    bands = jnp.linspace(1e-4, FILTER_BANDS - 1, FILTER_BANDS, dtype=F32)
    ang = 2.0 * math.pi * t[:, None] * bands[None, :]
    feat = jnp.concatenate([t[:, None], jnp.cos(ang), jnp.sin(ang)], axis=-1)
    fr = freq.astype(F32)
    hid = jnp.sin(fr * (feat @ w1.astype(F32) + b1.astype(F32)))
    hid = jnp.sin(fr * (hid @ w2.astype(F32) + b2.astype(F32)))
    h = (hid @ w3.astype(F32)).reshape(L, HYENA_ORDER, 2, W_H)
    deltas = jnp.abs(jnp.linspace(math.log(1e-2) / 1.5, math.log(1e-2) / 0.3, W_H, dtype=F32))
    h = h * jnp.exp(-t[:, None] * deltas[None, :])[:, None, None, :]
    return h * lax.rsqrt(jnp.sum(h * h, axis=(0, 2), keepdims=True) + EPS)


def trunk_layer(x, mod, p, B, L, rows, states):
    sh1, sc1, g1, sh2, sc2, g2 = mod
    mC, mn, mm, gS = states
    hb = norm_mod(x, p['norm_mix'], sc1, sh1, L)
    z = matmul(hb, p['w_in_main'], tm=1024, tn=512)
    zs = matmul(hb, p['w_in_small'], tm=1024, tn=N_SMALL_PAD)
    filt = hyena_filters(L, p['hyena_w1'], p['hyena_b1'], p['hyena_w2'], p['hyena_b2'], p['hyena_freq'], p['hyena_w3'])
    hyena = hyena_fft_pallas if L >= FFT_MIN_L else hyena_pallas
    y_a = hyena(z, p['hyena_conv_w'], filt, p['hyena_bias'], p['dft'][L], B, L)
    y_b, m_state = mlstm_pallas(z, zs, p['mlstm_gate_b'], p['mlstm_norm'],
                                None if mC is None else (mC, mn, mm), B, L)
    y_c, g_state = gdn_pallas(z, zs, p['gdn_conv_w'], p['gdn_A_log'], p['gdn_dt_bias'], p['gdn_norm'], gS, B, L)
    mix = branch_mix(y_a, y_b, y_c, p['w_branch'], z)
    x = matmul_residual(mix, p['w_out'], x, g1, L, tm=1024, tn=512)
    hb = norm_mod(x, p['norm_ffn'], sc2, sh2, L)
    gate_pre = matmul(hb, p['ffn_w_gate'], tm=1024, tn=1024)
    up = matmul(hb, p['ffn_w_up'], tm=1024, tn=1024)
    act = ffn_act(gate_pre, up, p['ffn_conv_w'], p['ffn_conv_b'], B, L, rows)
    x = matmul_residual(act, p['ffn_w_down'], x, g2, L, tm=1024, tn=512, tk=D_FF_PAD // 2)
    return x, m_state + (g_state,)


def _ada_all(c, c_ctx, ada_w, ada_b):
    cc = jnp.concatenate([c, c_ctx[None, :], jnp.zeros((7, D_MODEL), F32)], axis=0)
    a = jax.nn.silu(cc)
    outs = []
    for l in range(DEPTH):
        outs.append(matmul(a, ada_w[l], tm=16, tn=1024) + ada_b[l])
    return outs


def kernel(x_prompt, x_sample, state_mlstm_C, state_mlstm_n, state_mlstm_m, state_gdn_S, c, c_ctx,
           ada_w, ada_b, norm_mix, norm_ffn, w_in, hyena_conv_w, hyena_w1, hyena_b1, hyena_w2, hyena_b2,
           hyena_freq, hyena_w3, hyena_bias, mlstm_gate_b, mlstm_norm, gdn_conv_w, gdn_A_log, gdn_dt_bias,
           gdn_norm, w_branch, w_out, ffn_w_gate, ffn_w_up, ffn_conv_w, ffn_conv_b, ffn_w_down, final_norm):
    bp, lp, _ = x_prompt.shape
    nb, ls, _ = x_sample.shape
    rows_lat = ls // GRID_W
    zero_states = (None, None, None, None)
    ada = _ada_all(c, c_ctx, ada_w, ada_b)
    ffpad = D_FF_PAD - D_FF
    dft = {L: fft_matrices(L) if L >= FFT_MIN_L else dft_matrices(L) for L in sorted({lp, ls})}
    xp, xs = x_prompt.reshape(bp * lp, D_MODEL), x_sample.reshape(nb * ls, D_MODEL)
    out_C, out_n, out_m, out_S = [], [], [], []
    for l in range(DEPTH):
        wl = w_in[l]
        w_main = jnp.concatenate([wl[:, :_OFF[5]], wl[:, _OFF[6]:_OFF[8]], wl[:, _OFF[10]:]], axis=1).astype(BF16)
        w_small = jnp.concatenate([wl[:, _OFF[5]:_OFF[6]], wl[:, _OFF[8]:_OFF[10]],
                                   jnp.zeros((D_MODEL, N_SMALL_PAD - 96), F32)], axis=1).astype(BF16)
        p = {'dft': dft, 'norm_mix': norm_mix[l], 'norm_ffn': norm_ffn[l], 'w_in_main': w_main, 'w_in_small': w_small,
             'hyena_conv_w': hyena_conv_w[l], 'hyena_w1': hyena_w1[l], 'hyena_b1': hyena_b1[l],
             'hyena_w2': hyena_w2[l], 'hyena_b2': hyena_b2[l], 'hyena_freq': hyena_freq[l],
             'hyena_w3': hyena_w3[l], 'hyena_bias': hyena_bias[l],
             'mlstm_gate_b': mlstm_gate_b[l], 'mlstm_norm': mlstm_norm[l],
             'gdn_conv_w': gdn_conv_w[l], 'gdn_A_log': gdn_A_log[l], 'gdn_dt_bias': gdn_dt_bias[l],
             'gdn_norm': gdn_norm[l], 'w_branch': w_branch[l].astype(BF16), 'w_out': w_out[l].astype(BF16),
             'ffn_w_gate': jnp.pad(ffn_w_gate[l], ((0, 0), (0, ffpad))).astype(BF16),
             'ffn_w_up': jnp.pad(ffn_w_up[l], ((0, 0), (0, ffpad))).astype(BF16),
             'ffn_conv_w': jnp.pad(ffn_conv_w[l], ((0, 0), (0, 0), (0, ffpad))),
             'ffn_conv_b': jnp.pad(ffn_conv_b[l], ((0, ffpad),)),
             'ffn_w_down': jnp.pad(ffn_w_down[l], ((0, ffpad), (0, 0))).astype(BF16)}
        m = ada[l]
        mod_ctx = [t[None, None, :] for t in jnp.split(m[nb], 6, axis=-1)]
        mod_lat = [t[:, None, :] for t in jnp.split(m[:nb], 6, axis=-1)]
        xp, (sC, sn, sm, sS) = trunk_layer(xp, mod_ctx, p, bp, lp, 1, zero_states)
        out_C.append(sC)
        out_n.append(sn)
        out_m.append(sm)
        out_S.append(sS)
        cached = (state_mlstm_C[:, l].astype(F32), state_mlstm_n[:, l].astype(F32),
                  state_mlstm_m[:, l].astype(F32), state_gdn_S[:, l].astype(F32))
        xs, _ = trunk_layer(xs, mod_lat, p, nb, ls, rows_lat, cached)
    zero = jnp.zeros((1, 1, D_MODEL), F32)
    y_prompt = norm_mod(xp, final_norm, zero, zero, lp, out_dtype=F32).reshape(bp, lp, D_MODEL)
    y_sample = norm_mod(xs, final_norm, zero, zero, ls, out_dtype=F32).reshape(nb, ls, D_MODEL)
    return (y_prompt, y_sample, jnp.stack(out_C, axis=1), jnp.stack(out_n, axis=1),
            jnp.stack(out_m, axis=1), jnp.stack(out_S, axis=1))
```

```python
import functools
import math

import jax
import jax.numpy as jnp
import numpy as np
from jax import lax
from jax.experimental import pallas as pl
from jax.experimental.pallas import tpu as pltpu

D_MODEL = 4096
DEPTH = 2
GRID_W = 64
N_BRANCH = 3
BRANCH_W = D_MODEL // 2
W_H = BRANCH_W
HYENA_ORDER = 2
FILTER_BANDS = 16
H_M = 8
DV_M = BRANCH_W // H_M
DK_M = DV_M // 2
H_G = 16
DK_G = BRANCH_W // H_G
DV_G = BRANCH_W // H_G
D_FF = 11008
D_FF_PAD = 11264
CHUNK = 64
EPS = 1e-6
F32 = jnp.float32
BF16 = jnp.bfloat16

_IN_SPLITS = (3 * W_H, H_M * DK_M, H_M * DK_M, H_M * DV_M, H_M * DV_M, 4 * H_M,
              H_G * (2 * DK_G + DV_G), H_G * DV_G, 2 * H_G, 2 * H_G, N_BRANCH * D_MODEL)
_OFF = np.concatenate([[0], np.cumsum(_IN_SPLITS)]).tolist()
N_SMALL_PAD = 128

VMEM_LIMIT_BYTES = 56 * 1024 * 1024


def _mm_kernel(x_ref, w_ref, o_ref, *, nk):
    part = jnp.dot(x_ref[...].astype(BF16), w_ref[...].astype(BF16), preferred_element_type=F32)
    if nk == 1:
        o_ref[...] = part
    else:
        k = pl.program_id(2)

        @pl.when(k == 0)
        def _():
            o_ref[...] = part

        @pl.when(k != 0)
        def _():
            o_ref[...] += part


def matmul(x, w, *, tm, tn, tk=None):
    M, K = x.shape
    K2, N = w.shape
    assert K == K2
    tk = K if tk is None else tk
    tm = min(tm, M)
    assert M % tm == 0 and N % tn == 0 and K % tk == 0, (M, N, K, tm, tn, tk)
    nk = K // tk
    return pl.pallas_call(
        functools.partial(_mm_kernel, nk=nk),
        grid=(M // tm, N // tn, nk),
        in_specs=[pl.BlockSpec((tm, tk), lambda i, j, k: (i, k)),
                  pl.BlockSpec((tk, tn), lambda i, j, k: (k, j))],
        out_specs=pl.BlockSpec((tm, tn), lambda i, j, k: (i, j)),
        out_shape=jax.ShapeDtypeStruct((M, N), F32),
        compiler_params=pltpu.CompilerParams(
            dimension_semantics=("parallel", "parallel", "arbitrary"),
            vmem_limit_bytes=VMEM_LIMIT_BYTES),
    )(x, w)


def _dot(a, b):
    return jnp.dot(a.astype(BF16), b.astype(BF16), preferred_element_type=F32)


def _dot_nt(a, b):
    return lax.dot_general(a.astype(BF16), b.astype(BF16), (((1,), (1,)), ((), ())), preferred_element_type=F32)


def _dot_tn(a, b):
    return lax.dot_general(a.astype(BF16), b.astype(BF16), (((0,), (0,)), ((), ())), preferred_element_type=F32)


def _split2(a):
    hi = a.astype(BF16)
    return hi, (a - hi.astype(F32)).astype(BF16)


def _split3(a):
    hi = a.astype(BF16)
    r = a - hi.astype(F32)
    mid = r.astype(BF16)
    return hi, mid, (r - mid.astype(F32)).astype(BF16)


def _dot3(a, b):
    ah, al = _split2(a)
    bh, bl = _split2(b)
    return (jnp.dot(ah, bl, preferred_element_type=F32) + jnp.dot(al, bh, preferred_element_type=F32)
            + jnp.dot(ah, bh, preferred_element_type=F32))


def _dot_exact_lhs(a_bf16, b):
    b0, b1, b2 = _split3(b)
    return (jnp.dot(a_bf16, b2, preferred_element_type=F32) + jnp.dot(a_bf16, b1, preferred_element_type=F32)
            + jnp.dot(a_bf16, b0, preferred_element_type=F32))


def _inv_unit_triangular(ns, eye):
    xs = [eye - n for n in ns]
    ps = list(ns)
    for _ in range(5):
        ps = [_dot(p, p) for p in ps]
        xs = [x + _dot(x, p) for x, p in zip(xs, ps)]
    rs = [eye - x - _dot3(n, x) for n, x in zip(ns, xs)]
    return [x + _dot(x, r) for x, r in zip(xs, rs)]


def _softplus(x):
    return jnp.maximum(x, 0.0) + jnp.log1p(jnp.exp(-jnp.abs(x)))


def _lane_pick(x, lane):
    idx = lax.broadcasted_iota(jnp.int32, x.shape, 1)
    return jnp.sum(jnp.where(idx == lane, x, 0.0), axis=-1, keepdims=True)


def _conv3_rows(ref, w_ref, r0, nrows, total):
    x = ref[pl.ds(r0, nrows), :]
    row = lax.broadcasted_iota(jnp.int32, x.shape, 0)
    p0 = jnp.maximum(r0 - 8, 0)
    n0 = jnp.minimum(r0 + nrows, total - 8)
    prev = ref[pl.ds(pl.multiple_of(p0, 8), 8), :][7:8, :] * (r0 > 0).astype(F32)
    nxt = ref[pl.ds(pl.multiple_of(n0, 8), 8), :][0:1, :] * (r0 + nrows < total).astype(F32)
    xm1 = jnp.where(row == 0, prev, pltpu.roll(x, 1, 0))
    xp1 = jnp.where(row == nrows - 1, nxt, pltpu.roll(x, nrows - 1, 0))
    return xm1 * w_ref[0:1, :] + x * w_ref[1:2, :] + xp1 * w_ref[2:3, :]


GDN_LANE_BETA = 32
GDN_LANE_DECAY = 64
PREP_ROWS = 256
GDN_GROUP = 8


def _gdn_kernel(zq_ref, zk_ref, zv_ref, gz_ref, zs_ref, cwq_ref, cwk_ref, cwv_ref, alog_ref, dtb_ref, ng_ref,
                s0_ref, y_ref, sout_ref,
                q_s, k_s, v_s, gall_s, ball_s, o_s, wq_s, u_s, qkd_s, kd_s, eend_s, st_s, *, L, zero_init):
    h = pl.program_id(1)
    nc = L // CHUNK
    T = CHUNK
    row = lax.broadcasted_iota(jnp.int32, (T, T), 0)
    col = lax.broadcasted_iota(jnp.int32, (T, T), 1)
    eye = (row == col).astype(F32)
    ones_b = jnp.ones((T, T), BF16)
    incl = (row >= col, row <= col)
    strict = (row > col, row < col)

    @pl.when(h == 0)
    def _():
        lane = lax.broadcasted_iota(jnp.int32, (T, 128), 1)
        tril_b = incl[0].astype(BF16)
        triu_b = incl[1].astype(BF16)

        def seg(c, carry):
            r = pl.ds(pl.multiple_of(c * T, T), T)
            zs = zs_ref[r, :]
            ball_s[r, :] = jax.nn.sigmoid(zs)
            g = -jnp.exp(alog_ref[...]) * _softplus(zs + dtb_ref[...])
            pre = _dot_exact_lhs(tril_b, g)
            suf = _dot_exact_lhs(triu_b, g)
            gall_s[r, :] = jnp.where(lane < GDN_LANE_DECAY + H_G, pre, suf)
            return carry

        lax.fori_loop(0, nc, seg, 0)

    def prep(i, carry):
        r0 = pl.multiple_of(i * PREP_ROWS, PREP_ROWS)
        r = pl.ds(r0, PREP_ROWS)
        q = _conv3_rows(zq_ref, cwq_ref, r0, PREP_ROWS, L)
        q = q * jax.nn.sigmoid(q)
        q = q * lax.rsqrt(jnp.sum(q * q, axis=-1, keepdims=True) + EPS) * DK_G ** -0.5
        q_s[r, :] = q.astype(BF16)
        o_s[r, :] = jnp.zeros((PREP_ROWS, 128), F32)
        k = _conv3_rows(zk_ref, cwk_ref, r0, PREP_ROWS, L)
        k = k * jax.nn.sigmoid(k)
        k_s[r, :] = k * lax.rsqrt(jnp.sum(k * k, axis=-1, keepdims=True) + EPS)
        v = _conv3_rows(zv_ref, cwv_ref, r0, PREP_ROWS, L)
        v_s[r, :] = v * jax.nn.sigmoid(v)
        return carry

    lax.fori_loop(0, L // PREP_ROWS, prep, 0)

    grp = min(GDN_GROUP, nc)

    def phase_a(gi, carry):
        cs = [gi * grp + j for j in range(grp)]
        rs = [pl.ds(pl.multiple_of(c * T, T), T) for c in cs]
        qs = [q_s[r, :] for r in rs]
        ks = [k_s[r, :] for r in rs]
        vs = [v_s[r, :] for r in rs]
        kks = [_dot_nt(k, k) for k in ks]
        qks = [_dot_nt(q, k) for q, k in zip(qs, ks)]
        items = [(j, d) for j in range(grp) for d in range(2)]
        gcs = [_lane_pick(gall_s[rs[j], :], GDN_LANE_DECAY + d * H_G + h) for j, d in items]
        bcs = [_lane_pick(ball_s[rs[j], :], GDN_LANE_BETA + d * H_G + h) for j, d in items]
        gsq = [jnp.broadcast_to(g, (T, T)) for g in gcs]
        grs = [_dot_exact_lhs(ones_b, jnp.where(row == col, g, 0.0)) for g in gsq]
        decs = [jnp.exp(jnp.where(incl[d], g - gr, -1e30)) for (j, d), g, gr in zip(items, gsq, grs)]
        ns = [jnp.where(strict[d], b * kks[j] * dec, 0.0) for (j, d), b, dec in zip(items, bcs, decs)]
        tinvs = _inv_unit_triangular(ns, eye)
        egs = [jnp.exp(g) for g in gcs]
        ws = [_dot3(t, (b * eg) * ks[j]) for (j, d), t, b, eg in zip(items, tinvs, bcs, egs)]
        us = [_dot3(t, b * vs[j]) for (j, d), t, b in zip(items, tinvs, bcs)]
        for idx, (j, d) in enumerate(items):
            c, r, gc = cs[j], rs[j], gcs[idx]
            g_end = gc[T - 1:T, :] if d == 0 else gc[0:1, :]
            wq_s[d, pl.ds(pl.multiple_of(c * 2 * T, 2 * T), T), :] = ws[idx].astype(BF16)
            wq_s[d, pl.ds(pl.multiple_of(c * 2 * T, 2 * T) + T, T), :] = (egs[idx] * qs[j]).astype(BF16)
            u_s[d, r, :] = us[idx]
            qkd_s[d, r, :] = (qks[j] * decs[idx]).astype(BF16)
            kd_s[d, r, :] = (jnp.exp(g_end - gc) * ks[j]).astype(BF16)
            eend_s[d, pl.ds(pl.multiple_of(c * 8, 8), 8), :] = jnp.broadcast_to(jnp.exp(g_end), (8, 128))
        return carry

    lax.fori_loop(0, nc // grp, phase_a, 0)

    for d in range(2):
        if zero_init:
            st_s[d] = jnp.zeros((DK_G, DV_G), F32)
        else:
            st_s[d] = s0_ref[0, d, 0]

    def phase_b(i, carry):
        cs = (i, nc - 1 - i)
        rs = [pl.ds(pl.multiple_of(c * T, T), T) for c in cs]
        ss = [st_s[d] for d in range(2)]
        wss = [jnp.dot(wq_s[d, pl.ds(pl.multiple_of(cs[d] * 2 * T, 2 * T), 2 * T), :], ss[d].astype(BF16),
                       preferred_element_type=F32) for d in range(2)]
        vns = [(u_s[d, rs[d], :] - wss[d][:T]).astype(BF16) for d in range(2)]
        os_ = [jnp.dot(qkd_s[d, rs[d], :], vns[d], preferred_element_type=F32) for d in range(2)]
        kvs = [_dot_tn(kd_s[d, rs[d], :], vns[d]) for d in range(2)]
        for d in range(2):
            e_end = eend_s[d, pl.ds(pl.multiple_of(cs[d] * 8, 8), 8), :][0:1, :]
            st_s[d] = e_end * ss[d] + kvs[d]
        for d in range(2):
            o_s[rs[d], :] += wss[d][T:] + os_[d]
        return carry

    lax.fori_loop(0, nc, phase_b, 0)
    for d in range(2):
        sout_ref[0, d, 0] = st_s[d]

    def fin(i, carry):
        r = pl.ds(pl.multiple_of(i * PREP_ROWS, PREP_ROWS), PREP_ROWS)
        o = o_s[r, :]
        o = o * lax.rsqrt(jnp.mean(o * o, axis=-1, keepdims=True) + EPS) * ng_ref[...]
        z = gz_ref[r, :]
        y_ref[r, :] = (o * (z * jax.nn.sigmoid(z))).astype(y_ref.dtype)
        return carry

    lax.fori_loop(0, L // PREP_ROWS, fin, 0)


Z_COL_GQ = (3 * W_H + 2 * H_M * DK_M + 2 * H_M * DV_M) // 128
Z_COL_GZ = Z_COL_GQ + 3 * H_G


def gdn_pallas(z, zs, conv_w, a_log, dt_bias, norm_g, s0, B, L):
    zero_init = s0 is None
    if zero_init:
        s0 = jnp.zeros((1, 2, 1, DK_G, DV_G), F32)
        s0_map = lambda b, h: (0, 0, 0, 0, 0)
    else:
        s0_map = lambda b, h: (b, 0, h, 0, 0)
    lanes = jnp.zeros((1, 128), F32)
    alog_row = lanes.at[0, GDN_LANE_DECAY:GDN_LANE_DECAY + 2 * H_G].set(a_log.reshape(-1))
    dtb_row = lanes.at[0, GDN_LANE_DECAY:GDN_LANE_DECAY + 2 * H_G].set(dt_bias.reshape(-1))
    nc = L // CHUNK
    col = lambda off: pl.BlockSpec((L, 128), lambda b, h: (b, off + h))
    cw = lambda off: pl.BlockSpec((3, 128), lambda b, h: (0, off + h))
    one = pl.BlockSpec((1, 128), lambda b, h: (0, 0))
    st = lambda m: pl.BlockSpec((1, 2, 1, DK_G, DV_G), m)
    return pl.pallas_call(
        functools.partial(_gdn_kernel, L=L, zero_init=zero_init),
        grid=(B, H_G),
        in_specs=[col(Z_COL_GQ), col(Z_COL_GQ + H_G), col(Z_COL_GQ + 2 * H_G), col(Z_COL_GZ),
                  pl.BlockSpec((L, 128), lambda b, h: (b, 0)),
                  cw(0), cw(H_G), cw(2 * H_G), one, one, one, st(s0_map)],
        out_specs=[pl.BlockSpec((L, 128), lambda b, h: (b, h)), st(lambda b, h: (b, 0, h, 0, 0))],
        out_shape=[jax.ShapeDtypeStruct((B * L, H_G * DV_G), BF16),
                   jax.ShapeDtypeStruct((B, 2, H_G, DK_G, DV_G), F32)],
        scratch_shapes=[pltpu.VMEM((L, 128), BF16), pltpu.VMEM((L, 128), F32), pltpu.VMEM((L, 128), F32),
                        pltpu.VMEM((L, 128), F32), pltpu.VMEM((L, 128), F32), pltpu.VMEM((L, 128), F32),
                        pltpu.VMEM((2, 2 * L, 128), BF16), pltpu.VMEM((2, L, 128), F32),
                        pltpu.VMEM((2, L, CHUNK), BF16), pltpu.VMEM((2, L, 128), BF16),
                        pltpu.VMEM((2, nc * 8, 128), F32), pltpu.VMEM((2, DK_G, DV_G), F32)],
        compiler_params=pltpu.CompilerParams(
            dimension_semantics=("parallel", "arbitrary"), vmem_limit_bytes=VMEM_LIMIT_BYTES),
    )(z, z, z, z, zs, conv_w, conv_w, conv_w, alog_row, dtb_row, norm_g.reshape(1, DV_G), s0)


MLSTM_LANE_I = 0
MLSTM_LANE_F = 2 * H_M
NEG_BIG = -1e30


def _mlstm_kernel(q_ref, k_ref, v_ref, mo_ref, zs_ref, gb_ref, ng_ref, c0_ref, n0_ref, m0_ref,
                  y_ref, cout_ref, nout_ref, mout_ref, gall_s, h_s, c_s, n_s, m_s, *, L, zero_init):
    h = pl.program_id(1)
    nc = L // CHUNK
    T = CHUNK
    row = lax.broadcasted_iota(jnp.int32, (T, T), 0)
    col = lax.broadcasted_iota(jnp.int32, (T, T), 1)
    ones_b = jnp.ones((T, T), BF16)
    incl = (row >= col, row <= col)

    @pl.when(h == 0)
    def _():
        lane = lax.broadcasted_iota(jnp.int32, (T, 128), 1)
        tril_b = incl[0].astype(BF16)
        triu_b = incl[1].astype(BF16)

        def seg(c, carry):
            r = pl.ds(pl.multiple_of(c * T, T), T)
            gp = zs_ref[r, :] + gb_ref[...]
            lf = jnp.minimum(gp, 0.0) - jnp.log1p(jnp.exp(-jnp.abs(gp)))
            pre = _dot_exact_lhs(tril_b, lf)
            suf = _dot_exact_lhs(triu_b, lf)
            gall_s[r, :] = jnp.where(lane < MLSTM_LANE_F, gp, jnp.where(lane < MLSTM_LANE_F + H_M, pre, suf))
            return carry

        lax.fori_loop(0, nc, seg, 0)

    for d in range(2):
        if zero_init:
            c_s[d] = jnp.zeros((DK_M, DV_M), F32)
            n_s[d] = jnp.zeros((8, DK_M), F32)
            m_s[d] = jnp.zeros((8, 128), F32)
        else:
            c_s[d] = c0_ref[0, d, 0]
            n_s[d] = jnp.broadcast_to(n0_ref[0, d, 0], (8, DK_M))
            m_s[d] = jnp.broadcast_to(m0_ref[0, d, 0], (8, 128))

    def zero_h(i, carry):
        h_s[pl.ds(pl.multiple_of(i * PREP_ROWS, PREP_ROWS), PREP_ROWS), :] = jnp.zeros((PREP_ROWS, DV_M), F32)
        return carry

    lax.fori_loop(0, L // PREP_ROWS, zero_h, 0)

    def step(i, carry):
        D = range(2)
        cs = (i, nc - 1 - i)
        rs = [pl.ds(pl.multiple_of(c * T, T), T) for c in cs]
        qs = [q_ref[r, :] for r in rs]
        ks = [k_ref[r, :] * DK_M ** -0.5 for r in rs]
        vs = [v_ref[r, :].astype(BF16) for r in rs]
        qbs = [q.astype(BF16) for q in qs]
        qks = [_dot_nt(qbs[d], ks[d]) for d in D]
        cst = [c_s[d] for d in D]
        qcs = [jnp.dot(qbs[d], cst[d].astype(BF16), preferred_element_type=F32) for d in D]
        bcol = [_lane_pick(gall_s[rs[d], :], MLSTM_LANE_F + d * H_M + h) for d in D]
        icol = [_lane_pick(gall_s[rs[d], :], MLSTM_LANE_I + d * H_M + h) for d in D]
        amb = [icol[d] - bcol[d] for d in D]
        rmat = [_dot_exact_lhs(ones_b, jnp.where(row == col, jnp.broadcast_to(amb[d], (T, T)), 0.0)) for d in D]
        dmat = [jnp.where(incl[d], bcol[d] + rmat[d], NEG_BIG) for d in D]
        m = [m_s[d][0:1, 0:1] for d in D]
        inter = [bcol[d] + m[d] for d in D]
        m_t = [jnp.maximum(inter[d], jnp.max(dmat[d], axis=-1, keepdims=True)) for d in D]
        smat = [qks[d] * jnp.exp(dmat[d] - m_t[d]) for d in D]
        w_inter = [jnp.exp(inter[d] - m_t[d]) for d in D]
        sv = [jnp.dot(smat[d].astype(BF16), vs[d], preferred_element_type=F32) for d in D]
        b_end = [bcol[0][T - 1:T, :], bcol[1][0:1, :]]
        gcol = [b_end[d] + amb[d] for d in D]
        m_new = [jnp.maximum(b_end[d] + m[d], jnp.max(gcol[d], axis=0, keepdims=True)) for d in D]
        kw = [jnp.exp(gcol[d] - m_new[d]) * ks[d] for d in D]
        kv = [_dot_tn(kw[d], vs[d]) for d in D]
        for d in D:
            n = n_s[d][0:1, :]
            cd = jnp.exp(b_end[d] + m[d] - m_new[d])
            num = sv[d] + w_inter[d] * qcs[d]
            den = (jnp.sum(smat[d], axis=-1, keepdims=True)
                   + w_inter[d] * jnp.sum(qs[d] * n, axis=-1, keepdims=True))
            h_s[rs[d], :] += num / jnp.maximum(jnp.abs(den), jnp.exp(-m_t[d]))
            c_s[d] = cd * cst[d] + kv[d]
            n_s[d] = jnp.broadcast_to(cd * n + jnp.sum(kw[d], axis=0, keepdims=True), (8, DK_M))
            m_s[d] = jnp.broadcast_to(m_new[d], (8, 128))
        return carry

    lax.fori_loop(0, nc, step, 0)
    for d in range(2):
        cout_ref[0, d, 0] = c_s[d]
        nout_ref[0, d, 0] = n_s[d][0:1, :]
        mout_ref[0, d, 0] = m_s[d][0:1, 0:1]

    def fin(i, carry):
        r = pl.ds(pl.multiple_of(i * PREP_ROWS, PREP_ROWS), PREP_ROWS)
        o = h_s[r, :]
        o = o * lax.rsqrt(jnp.mean(o * o, axis=-1, keepdims=True) + EPS) * ng_ref[0]
        y_ref[r, :] = (o * jax.nn.sigmoid(mo_ref[r, :])).astype(y_ref.dtype)
        return carry

    lax.fori_loop(0, L // PREP_ROWS, fin, 0)


Z_COL_MQ = 3 * W_H // 128
Z_COL_MV = (3 * W_H + 2 * H_M * DK_M) // DV_M


def mlstm_pallas(z, zs, gate_b, norm_g, state0, B, L):
    zero_init = state0 is None
    if zero_init:
        c0 = jnp.zeros((1, 2, 1, DK_M, DV_M), F32)
        n0 = jnp.zeros((1, 2, 1, 1, DK_M), F32)
        m0 = jnp.zeros((1, 2, 1, 1, 1), F32)
        smap = lambda b, h: (0, 0, 0, 0, 0)
    else:
        c0 = state0[0]
        n0 = state0[1].reshape(B, 2, H_M, 1, DK_M)
        m0 = state0[2].reshape(B, 2, H_M, 1, 1)
        smap = lambda b, h: (b, 0, h, 0, 0)
    omap = lambda b, h: (b, 0, h, 0, 0)
    gb_row = jnp.zeros((1, 128), F32).at[0, :4 * H_M].set(gate_b.reshape(-1))
    st_specs = lambda m: [pl.BlockSpec((1, 2, 1, DK_M, DV_M), m), pl.BlockSpec((1, 2, 1, 1, DK_M), m),
                          pl.BlockSpec((1, 2, 1, 1, 1), m)]
    y, c, n, m = pl.pallas_call(
        functools.partial(_mlstm_kernel, L=L, zero_init=zero_init),
        grid=(B, H_M),
        in_specs=[pl.BlockSpec((L, DK_M), lambda b, h: (b, Z_COL_MQ + h)),
                  pl.BlockSpec((L, DK_M), lambda b, h: (b, Z_COL_MQ + H_M + h)),
                  pl.BlockSpec((L, DV_M), lambda b, h: (b, Z_COL_MV + h)),
                  pl.BlockSpec((L, DV_M), lambda b, h: (b, Z_COL_MV + H_M + h)),
                  pl.BlockSpec((L, 128), lambda b, h: (b, 0)),
                  pl.BlockSpec((1, 128), lambda b, h: (0, 0)),
                  pl.BlockSpec((1, 1, DV_M), lambda b, h: (h, 0, 0))] + st_specs(smap),
        out_specs=[pl.BlockSpec((L, DV_M), lambda b, h: (b, h))] + st_specs(omap),
        out_shape=[jax.ShapeDtypeStruct((B * L, H_M * DV_M), BF16),
                   jax.ShapeDtypeStruct((B, 2, H_M, DK_M, DV_M), F32),
                   jax.ShapeDtypeStruct((B, 2, H_M, 1, DK_M), F32),
                   jax.ShapeDtypeStruct((B, 2, H_M, 1, 1), F32)],
        scratch_shapes=[pltpu.VMEM((L, 128), F32), pltpu.VMEM((L, DV_M), F32),
                        pltpu.VMEM((2, DK_M, DV_M), F32), pltpu.VMEM((2, 8, DK_M), F32),
                        pltpu.VMEM((2, 8, 128), F32)],
        compiler_params=pltpu.CompilerParams(
            dimension_semantics=("parallel", "arbitrary"), vmem_limit_bytes=VMEM_LIMIT_BYTES),
    )(z, z, z, z, zs, gb_row, norm_g.reshape(H_M, 1, DV_M), c0, n0, m0)
    return y, (c, n.reshape(B, 2, H_M, DK_M), m.reshape(B, 2, H_M))


FBLK = 256
HY_TN = 512
HY_CONV_TN = 128


def dft_matrices(L):
    n = 2 * L
    nf = L + 1
    nfb = -(-nf // FBLK)
    nfp = nfb * FBLK
    f = jnp.arange(nfp, dtype=jnp.int32)
    t = jnp.arange(L, dtype=jnp.int32)
    ang = ((f[:, None] * t[None, :]) % n).astype(F32) * (2.0 * math.pi / n)
    valid = (f < nf)[:, None]
    c = jnp.where(valid, jnp.cos(ang), 0.0)
    s = jnp.where(valid, jnp.sin(ang), 0.0)
    fwd = jnp.concatenate([c.reshape(nfb, FBLK, L), -s.reshape(nfb, FBLK, L)], axis=1).reshape(2 * nfp, L)
    wgt = jnp.where((f == 0) | (f == L), 1.0, 2.0)[:, None] / n
    inv = jnp.concatenate([(wgt * c).reshape(nfb, FBLK, L), (-wgt * s).reshape(nfb, FBLK, L)], axis=1)
    return fwd, inv.reshape(2 * nfp, L).T


def _hy_conv_kernel(v_ref, x1_ref, x2_ref, wv_ref, w1_ref, w2_ref, v32_ref, vb_ref, x1o_ref, x2o_ref, *, L):
    def body(i, carry):
        r0 = pl.multiple_of(i * PREP_ROWS, PREP_ROWS)
        r = pl.ds(r0, PREP_ROWS)
        v = _conv3_rows(v_ref, wv_ref, r0, PREP_ROWS, L)
        v32_ref[r, :] = v
        vb_ref[r, :] = v.astype(BF16)
        x1o_ref[r, :] = _conv3_rows(x1_ref, w1_ref, r0, PREP_ROWS, L)
        x2o_ref[r, :] = _conv3_rows(x2_ref, w2_ref, r0, PREP_ROWS, L)
        return carry

    lax.fori_loop(0, L // PREP_ROWS, body, 0)


def hyena_short_conv(z, conv_w, B, L):
    nj = W_H // HY_CONV_TN
    col = lambda off: pl.BlockSpec((L, HY_CONV_TN), lambda b, j: (b, off * nj + j))
    cw = lambda off: pl.BlockSpec((3, HY_CONV_TN), lambda b, j: (0, off * nj + j))
    out = pl.BlockSpec((L, HY_CONV_TN), lambda b, j: (b, j))
    f32o = jax.ShapeDtypeStruct((B * L, W_H), F32)
    return pl.pallas_call(
        functools.partial(_hy_conv_kernel, L=L),
        grid=(B, nj),
        in_specs=[col(0), col(1), col(2), cw(0), cw(1), cw(2)],
        out_specs=[out, out, out, out],
        out_shape=[f32o, jax.ShapeDtypeStruct((B * L, W_H), BF16), f32o, f32o],
        compiler_params=pltpu.CompilerParams(
            dimension_semantics=("parallel", "parallel"), vmem_limit_bytes=VMEM_LIMIT_BYTES),
    )(z, z, z, conv_w, conv_w, conv_w)


def _dft_fwd_kernel(f_ref, u_ref, k_ref, o_ref):
    acc = jnp.dot(f_ref[...], u_ref[...], preferred_element_type=F32)
    re, im = acc[:FBLK], acc[FBLK:]
    kre, kim = k_ref[:FBLK, :], k_ref[FBLK:, :]
    o_ref[:FBLK, :] = (re * kre - im * kim).astype(o_ref.dtype)
    o_ref[FBLK:, :] = (re * kim + im * kre).astype(o_ref.dtype)


def dft_filter_fwd(fwd_b, ub, kspec, B, L):
    m2 = fwd_b.shape[0]
    nfb = m2 // (2 * FBLK)
    C = ub.shape[1]
    return pl.pallas_call(
        _dft_fwd_kernel,
        grid=(B, C // HY_TN, nfb),
        in_specs=[pl.BlockSpec((2 * FBLK, L), lambda b, j, i: (i, 0)),
                  pl.BlockSpec((L, HY_TN), lambda b, j, i: (b, j)),
                  pl.BlockSpec((2 * FBLK, HY_TN), lambda b, j, i: (i, j))],
        out_specs=pl.BlockSpec((2 * FBLK, HY_TN), lambda b, j, i: (b * nfb + i, j)),
        out_shape=jax.ShapeDtypeStruct((B * m2, C), BF16),
        compiler_params=pltpu.CompilerParams(
            dimension_semantics=("parallel", "parallel", "arbitrary"), vmem_limit_bytes=VMEM_LIMIT_BYTES),
    )(fwd_b, ub, kspec)


def _dft_inv_kernel(g_ref, s_ref, xg_ref, u_ref, b_ref, *o_refs):
    y = jnp.dot(g_ref[...], s_ref[...], preferred_element_type=F32)
    u = u_ref[...]
    out = xg_ref[...] * (y + u * b_ref[...])
    for o_ref in o_refs:
        o_ref[...] = out.astype(o_ref.dtype)


def dft_inv_gate(inv_b, spec, xg, u, bias, B, L, out_dtypes):
    m2 = inv_b.shape[1]
    C = spec.shape[1]
    tm = min(512, L)
    blk = pl.BlockSpec((tm, HY_TN), lambda b, j, i: (b * (L // tm) + i, j))
    return pl.pallas_call(
        _dft_inv_kernel,
        grid=(B, C // HY_TN, L // tm),
        in_specs=[pl.BlockSpec((tm, m2), lambda b, j, i: (i, 0)),
                  pl.BlockSpec((m2, HY_TN), lambda b, j, i: (b, j)),
                  blk, blk, pl.BlockSpec((1, HY_TN), lambda b, j, i: (0, j))],
        out_specs=[blk for _ in out_dtypes],
        out_shape=[jax.ShapeDtypeStruct((B * L, C), dt) for dt in out_dtypes],
        compiler_params=pltpu.CompilerParams(
            dimension_semantics=("parallel", "parallel", "arbitrary"), vmem_limit_bytes=VMEM_LIMIT_BYTES),
    )(inv_b, spec, xg, u, bias)


def hyena_filter_spectra(filt, fwd):
    L = filt.shape[0]
    kf = filt[:, :, 0].at[0].add(filt[0, :, 1])
    kb = filt[:, :, 1].at[0].set(0.0)
    k_all = jnp.concatenate([kf, kb], axis=1).reshape(L, 2 * HYENA_ORDER * W_H)
    f_hi, f_lo = _split2(fwd)
    k_hi, k_lo = _split2(k_all)
    mm = functools.partial(matmul, tm=2 * FBLK, tn=512)
    spec = mm(f_hi, k_lo) + mm(f_lo, k_hi) + mm(f_hi, k_hi)
    spec = spec.reshape(-1, 2 * HYENA_ORDER, W_H)
    sf, sb = spec[:, :HYENA_ORDER], spec[:, HYENA_ORDER:]
    is_im = ((jnp.arange(spec.shape[0]) // FBLK) % 2 == 1)[:, None, None]
    ks = sf + jnp.where(is_im, -sb, sb)
    return [ks[:, o] for o in range(HYENA_ORDER)]


def hyena_pallas(z, conv_w, filt, bias, mats, B, L):
    fwd, inv = mats
    fwd_b, inv_b = fwd.astype(BF16), inv.astype(BF16)
    k0, k1 = hyena_filter_spectra(filt, fwd)
    v32, vb, x1, x2 = hyena_short_conv(z, conv_w, B, L)
    b0 = bias[0].reshape(1, W_H).astype(F32)
    b1 = bias[1].reshape(1, W_H).astype(F32)
    spec = dft_filter_fwd(fwd_b, vb, k0, B, L)
    s32, sb = dft_inv_gate(inv_b, spec, x1, v32, b0, B, L, (F32, BF16))
    spec = dft_filter_fwd(fwd_b, sb, k1, B, L)
    (y,) = dft_inv_gate(inv_b, spec, x2, s32, b1, B, L, (BF16,))
    return y


FFT_N2 = 128
FFT_T2_ROWS = 8
FFT_MIN_L = 8 * FFT_N2


def fft_matrices(L):
    n = 2 * L
    n1 = n // FFT_N2
    f1 = jnp.arange(n1, dtype=jnp.int32)
    t1 = jnp.arange(n1 // 2, dtype=jnp.int32)
    t2 = jnp.arange(FFT_N2, dtype=jnp.int32)
    ang_a = ((f1[:, None] * t1[None, :]) % n1).astype(F32) * (2.0 * math.pi / n1)
    ca, sa = jnp.cos(ang_a), jnp.sin(ang_a)
    m_a = jnp.stack([ca, -sa], axis=1).reshape(2 * n1, n1 // 2)
    m_c = m_a.T / n
    ang_t = (f1[:, None] * t2[None, :]).astype(F32) * (2.0 * math.pi / n)
    twc = jnp.cos(ang_t).reshape(n1 * FFT_N2, 1)
    tws = jnp.sin(ang_t).reshape(n1 * FFT_N2, 1)
    ang_2 = ((t2[:, None] * t2[None, :]) % FFT_N2).astype(F32) * (2.0 * math.pi / FFT_N2)
    c2, s2 = jnp.cos(ang_2), jnp.sin(ang_2)
    m_f = jnp.concatenate([jnp.concatenate([c2, s2], axis=1), jnp.concatenate([-s2, c2], axis=1)], axis=0)
    m_i = jnp.concatenate([jnp.concatenate([c2, -s2], axis=1), jnp.concatenate([s2, c2], axis=1)], axis=0)
    return dict(n1=n1, m_a=m_a, m_c=m_c, twc=twc, tws=tws, m_f=m_f, m_i=m_i)


def _mm_any(a, b, precise):
    if precise:
        return _dot3(a, b)
    return jnp.dot(a.astype(BF16), b.astype(BF16), preferred_element_type=F32)


def _fft_a_kernel(m_ref, u_ref, o_ref, *, precise):
    ut = jnp.swapaxes(u_ref[...], 0, 1)
    o = jnp.stack([_mm_any(m_ref[...], ut[s], precise) for s in range(FFT_T2_ROWS)], axis=0)
    o_ref[...] = jnp.swapaxes(o, 0, 1)


def fft_stage_a(m_a, u, nb, L, precise=False):
    C = u.shape[1]
    n1 = m_a.shape[0] // 2
    out = pl.pallas_call(
        functools.partial(_fft_a_kernel, precise=precise),
        grid=(nb, FFT_N2 // FFT_T2_ROWS),
        in_specs=[pl.BlockSpec((2 * n1, n1 // 2), lambda b, j: (0, 0)),
                  pl.BlockSpec((n1 // 2, FFT_T2_ROWS, C), lambda b, j: (b, j, 0))],
        out_specs=pl.BlockSpec((2 * n1, FFT_T2_ROWS, C), lambda b, j: (b, j, 0)),
        out_shape=jax.ShapeDtypeStruct((nb * 2 * n1, FFT_N2, C), F32),
        compiler_params=pltpu.CompilerParams(
            dimension_semantics=("parallel", "parallel"), vmem_limit_bytes=VMEM_LIMIT_BYTES),
    )(m_a if precise else m_a.astype(BF16), u.reshape(nb * (n1 // 2), FFT_N2, C))
    return out.reshape(nb * 2 * n1 * FFT_N2, C)


def _fft_mid_kernel(a_ref, twc_ref, tws_ref, mf_ref, mi_ref, k_ref, o_ref, *, inverse, precise):
    n2 = FFT_N2
    c, s = twc_ref[...], tws_ref[...]
    are, aim = a_ref[:n2, :], a_ref[n2:, :]
    a = jnp.concatenate([are * c + aim * s, aim * c - are * s], axis=0)
    x = _mm_any(mf_ref[...], a, precise)
    if not inverse:
        o_ref[...] = x
        return
    xre, xim = x[:n2], x[n2:]
    kre, kim = k_ref[:n2, :], k_ref[n2:, :]
    y = jnp.concatenate([xre * kre - xim * kim, xre * kim + xim * kre], axis=0)
    bp = _mm_any(mi_ref[...], y, precise)
    bre, bim = bp[:n2], bp[n2:]
    o_ref[:n2, :] = (bre * c - bim * s).astype(o_ref.dtype)
    o_ref[n2:, :] = (bre * s + bim * c).astype(o_ref.dtype)


def fft_stage_mid(a, mats, kspec, nb, inverse=True, precise=False):
    C = a.shape[1]
    n1 = mats['n1']
    slab = pl.BlockSpec((2 * FFT_N2, C), lambda b, f: (b * n1 + f, 0))
    tw = pl.BlockSpec((FFT_N2, 1), lambda b, f: (f, 0))
    mat = pl.BlockSpec((2 * FFT_N2, 2 * FFT_N2), lambda b, f: (0, 0))
    cast = (lambda m: m) if precise else (lambda m: m.astype(BF16))
    if kspec is None:
        kspec = jnp.zeros((2 * FFT_N2, C), F32)
        kmap = lambda b, f: (0, 0)
    else:
        kmap = lambda b, f: (f, 0)
    return pl.pallas_call(
        functools.partial(_fft_mid_kernel, inverse=inverse, precise=precise),
        grid=(nb, n1),
        in_specs=[slab, tw, tw, mat, mat, pl.BlockSpec((2 * FFT_N2, C), kmap)],
        out_specs=slab,
        out_shape=jax.ShapeDtypeStruct(a.shape, F32),
        compiler_params=pltpu.CompilerParams(
            dimension_semantics=("parallel", "parallel"), vmem_limit_bytes=VMEM_LIMIT_BYTES),
    )(a, mats['twc'], mats['tws'], cast(mats['m_f']), cast(mats['m_i']), kspec)


def _fft_c_kernel(m_ref, b_ref, xg_ref, u_ref, bias_ref, *o_refs):
    bt = jnp.swapaxes(b_ref[...], 0, 1)
    y = jnp.stack([jnp.dot(m_ref[...], bt[s].astype(BF16), preferred_element_type=F32)
                   for s in range(FFT_T2_ROWS)], axis=0)
    out = xg_ref[...] * (jnp.swapaxes(y, 0, 1) + u_ref[...] * bias_ref[...])
    for o_ref in o_refs:
        o_ref[...] = out.astype(o_ref.dtype)


def fft_stage_c(m_c, bsp, xg, u, bias, nb, L, out_dtypes):
    C = xg.shape[1]
    n1 = m_c.shape[1] // 2
    v3 = lambda a, rows: a.reshape(nb * rows, FFT_N2, C)
    blk = pl.BlockSpec((n1 // 2, FFT_T2_ROWS, C), lambda b, j: (b, j, 0))
    outs = pl.pallas_call(
        _fft_c_kernel,
        grid=(nb, FFT_N2 // FFT_T2_ROWS),
        in_specs=[pl.BlockSpec((n1 // 2, 2 * n1), lambda b, j: (0, 0)),
                  pl.BlockSpec((2 * n1, FFT_T2_ROWS, C), lambda b, j: (b, j, 0)),
                  blk, blk, pl.BlockSpec((1, C), lambda b, j: (0, 0))],
        out_specs=[blk for _ in out_dtypes],
        out_shape=[jax.ShapeDtypeStruct((nb * (n1 // 2), FFT_N2, C), dt) for dt in out_dtypes],
        compiler_params=pltpu.CompilerParams(
            dimension_semantics=("parallel", "parallel"), vmem_limit_bytes=VMEM_LIMIT_BYTES),
    )(m_c.astype(BF16), v3(bsp, 2 * n1), v3(xg, n1 // 2), v3(u, n1 // 2), bias)
    return [o.reshape(nb * L, C) for o in outs]


def fft_filter_spectra(filt, mats):
    L = filt.shape[0]
    kf = filt[:, :, 0].at[0].add(filt[0, :, 1])
    kb = filt[:, :, 1].at[0].set(0.0)
    nsig = 2 * HYENA_ORDER
    k_all = jnp.moveaxis(jnp.concatenate([kf, kb], axis=1), 1, 0).reshape(nsig * L, W_H)
    a = fft_stage_a(mats['m_a'], k_all, nsig, L, precise=True)
    spec = fft_stage_mid(a, mats, None, nsig, inverse=False, precise=True).reshape(nsig, -1, W_H)
    is_im = ((jnp.arange(spec.shape[1]) // FFT_N2) % 2 == 1)[None, :, None]
    sf, sb = spec[:HYENA_ORDER], spec[HYENA_ORDER:]
    ks = sf + jnp.where(is_im, -sb, sb)
    return [ks[o] for o in range(HYENA_ORDER)]


def hyena_fft_pallas(z, conv_w, filt, bias, mats, B, L):
    k0, k1 = fft_filter_spectra(filt, mats)
    v32, _, x1, x2 = hyena_short_conv(z, conv_w, B, L)
    b0 = bias[0].reshape(1, W_H).astype(F32)
    b1 = bias[1].reshape(1, W_H).astype(F32)
    a = fft_stage_a(mats['m_a'], v32, B, L)
    bs = fft_stage_mid(a, mats, k0, B)
    (s32,) = fft_stage_c(mats['m_c'], bs, x1, v32, b0, B, L, (F32,))
    a = fft_stage_a(mats['m_a'], s32, B, L)
    bs = fft_stage_mid(a, mats, k1, B)
    (y,) = fft_stage_c(mats['m_c'], bs, x2, s32, b1, B, L, (F32,))
    return y.astype(BF16)


def _norm_mod_kernel(x_ref, g_ref, sc_ref, sh_ref, o_ref):
    x = x_ref[...]
    y = x * lax.rsqrt(jnp.mean(x * x, axis=-1, keepdims=True) + EPS)
    o_ref[...] = ((y * g_ref[...]) * (1.0 + sc_ref[0]) + sh_ref[0]).astype(o_ref.dtype)


def norm_mod(x, g, sc, sh, L, out_dtype=BF16):
    T = x.shape[0]
    tm = 256
    per_row = sc.shape[0] > 1
    mod = pl.BlockSpec((1, 1, D_MODEL), (lambda i: (i // (L // tm), 0, 0)) if per_row else (lambda i: (0, 0, 0)))
    return pl.pallas_call(
        _norm_mod_kernel,
        grid=(T // tm,),
        in_specs=[pl.BlockSpec((tm, D_MODEL), lambda i: (i, 0)), pl.BlockSpec((1, D_MODEL), lambda i: (0, 0)),
                  mod, mod],
        out_specs=pl.BlockSpec((tm, D_MODEL), lambda i: (i, 0)),
        out_shape=jax.ShapeDtypeStruct((T, D_MODEL), out_dtype),
        compiler_params=pltpu.CompilerParams(dimension_semantics=("parallel",), vmem_limit_bytes=VMEM_LIMIT_BYTES),
    )(x, g.reshape(1, D_MODEL), sc, sh)


def _branch_mix_kernel(ya_ref, yb_ref, yc_ref, wb_ref, za_ref, zb_ref, zc_ref, o_ref):
    acc = None
    for i, (y_ref, zg_ref) in enumerate(((ya_ref, za_ref), (yb_ref, zb_ref), (yc_ref, zc_ref))):
        t = jax.nn.sigmoid(zg_ref[...]) * jnp.dot(y_ref[...], wb_ref[i], preferred_element_type=F32)
        acc = t if acc is None else acc + t
    o_ref[...] = acc.astype(o_ref.dtype)


Z_COL_ZM = (3 * W_H + 2 * H_M * DK_M + 2 * H_M * DV_M + H_G * (2 * DK_G + DV_G) + H_G * DV_G)


def branch_mix(ya, yb, yc, wb, z):
    T = ya.shape[0]
    tm, tn = 512, 512
    nj = D_MODEL // tn
    yspec = pl.BlockSpec((tm, BRANCH_W), lambda i, j: (i, 0))
    zspec = lambda br: pl.BlockSpec((tm, tn), lambda i, j: (i, Z_COL_ZM // tn + br * nj + j))
    return pl.pallas_call(
        _branch_mix_kernel,
        grid=(T // tm, nj),
        in_specs=[yspec, yspec, yspec, pl.BlockSpec((N_BRANCH, BRANCH_W, tn), lambda i, j: (0, 0, j)),
                  zspec(0), zspec(1), zspec(2)],
        out_specs=pl.BlockSpec((tm, tn), lambda i, j: (i, j)),
        out_shape=jax.ShapeDtypeStruct((T, D_MODEL), BF16),
        compiler_params=pltpu.CompilerParams(
            dimension_semantics=("parallel", "parallel"), vmem_limit_bytes=VMEM_LIMIT_BYTES),
    )(ya, yb, yc, wb, z, z, z)


def _mm_res_kernel(a_ref, w_ref, r_ref, g_ref, o_ref, *, nk):
    part = jnp.dot(a_ref[...], w_ref[...], preferred_element_type=F32)
    if nk == 1:
        o_ref[...] = r_ref[...] + g_ref[0] * part
    else:
        k = pl.program_id(2)

        @pl.when(k == 0)
        def _():
            o_ref[...] = part

        @pl.when((k != 0) & (k != nk - 1))
        def _():
            o_ref[...] += part

        @pl.when(k == nk - 1)
        def _():
            o_ref[...] = r_ref[...] + g_ref[0] * (o_ref[...] + part)


def matmul_residual(a, w, res, gate, L, *, tm, tn, tk=None):
    T, K = a.shape
    N = w.shape[1]
    tk = K if tk is None else tk
    nk = K // tk
    per_row = gate.shape[0] > 1
    tm = min(tm, L) if per_row else tm
    assert T % tm == 0 and N % tn == 0 and K % tk == 0 and (L % tm == 0 or not per_row), (T, N, K, L, tm, tn, tk)
    gspec = pl.BlockSpec((1, 1, tn), (lambda i, j, k: (i // (L // tm), 0, j)) if per_row else (lambda i, j, k: (0, 0, j)))
    return pl.pallas_call(
        functools.partial(_mm_res_kernel, nk=nk),
        grid=(T // tm, N // tn, nk),
        in_specs=[pl.BlockSpec((tm, tk), lambda i, j, k: (i, k)), pl.BlockSpec((tk, tn), lambda i, j, k: (k, j)),
                  pl.BlockSpec((tm, tn), lambda i, j, k: (i, j)), gspec],
        out_specs=pl.BlockSpec((tm, tn), lambda i, j, k: (i, j)),
        out_shape=jax.ShapeDtypeStruct((T, N), F32),
        compiler_params=pltpu.CompilerParams(
            dimension_semantics=("parallel", "parallel", "arbitrary"), vmem_limit_bytes=VMEM_LIMIT_BYTES),
    )(a, w, res, gate)


FFN_ROWS = 256
FFN_HALO = 128
FFN_CG = 256
FFN_BLOCK_ELEMS = 1 << 21


def _ffn_act_kernel(gp_ref, up_ref, w_ref, b_ref, o_ref, *, L, rows):
    width = L // rows
    wshift = width.bit_length() - 1
    n_ext = FFN_ROWS + 2 * FFN_HALO
    drs = (-1, 0, 1) if rows > 1 else (0,)
    assert rows == 1 or (width % 8 == 0 and width + 1 <= FFN_HALO)

    def body(c, carry):
        r0 = pl.multiple_of(c * FFN_ROWS, FFN_ROWS)
        p0 = pl.multiple_of(jnp.maximum(r0 - FFN_HALO, 0), FFN_HALO)
        n0 = pl.multiple_of(jnp.minimum(r0 + FFN_ROWS, L - FFN_HALO), FFN_HALO)
        r = pl.ds(r0, FFN_ROWS)
        tok = r0 + lax.broadcasted_iota(jnp.int32, (FFN_ROWS, 1), 0)
        gr = tok >> wshift
        gw = tok & (width - 1)
        ok_r = {dr: (gr + dr >= 0) & (gr + dr < rows) for dr in drs}
        ok_w = {dw: (gw + dw >= 0) & (gw + dw < width) for dw in (-1, 0, 1)}
        for cg in range(o_ref.shape[1] // FFN_CG):
            cols = slice(cg * FFN_CG, (cg + 1) * FFN_CG)
            ext = jnp.concatenate([gp_ref[pl.ds(p0, FFN_HALO), cols], gp_ref[r, cols],
                                   gp_ref[pl.ds(n0, FFN_HALO), cols]], axis=0)
            acc = jnp.zeros((FFN_ROWS, FFN_CG), F32) + b_ref[:, cols]
            for dw in (-1, 0, 1):
                e = ext if dw == 0 else pltpu.roll(ext, (-dw) % n_ext, 0)
                for dr in drs:
                    lo = FFN_HALO + dr * width
                    k = 3 * (dr + 1) + (dw + 1)
                    acc = acc + jnp.where(ok_r[dr] & ok_w[dw], e[lo:lo + FFN_ROWS], 0.0) * w_ref[k:k + 1, cols]
            o_ref[r, cols] = ((acc * jax.nn.sigmoid(acc)) * up_ref[r, cols]).astype(o_ref.dtype)
        return carry

    lax.fori_loop(0, L // FFN_ROWS, body, 0)


def ffn_act(gate_pre, up, conv_w, conv_b, B, L, rows):
    C = gate_pre.shape[1]
    ct = min(1024, max(FFN_CG, FFN_BLOCK_ELEMS // L))
    blk = pl.BlockSpec((L, ct), lambda b, j: (b, j))
    return pl.pallas_call(
        functools.partial(_ffn_act_kernel, L=L, rows=rows),
        grid=(B, C // ct),
        in_specs=[blk, blk, pl.BlockSpec((9, ct), lambda b, j: (0, j)),
                  pl.BlockSpec((1, ct), lambda b, j: (0, j))],
        out_specs=blk,
        out_shape=jax.ShapeDtypeStruct((B * L, C), BF16),
        compiler_params=pltpu.CompilerParams(
            dimension_semantics=("parallel", "parallel"), vmem_limit_bytes=VMEM_LIMIT_BYTES),
    )(gate_pre, up, conv_w.reshape(9, C), conv_b.reshape(1, C))


def hyena_filters(L, w1, b1, w2, b2, freq, w3):
    t = jnp.arange(L, dtype=F32) / L
    bands = jnp.linspace(1e-4, FILTER_BANDS - 1, FILTER_BANDS, dtype=F32)
    ang = 2.0 * math.pi * t[:, None] * bands[None, :]
    feat = jnp.concatenate([t[:, None], jnp.cos(ang), jnp.sin(ang)], axis=-1)
    fr = freq.astype(F32)
    hid = jnp.sin(fr * (feat @ w1.astype(F32) + b1.astype(F32)))
    hid = jnp.sin(fr * (hid @ w2.astype(F32) + b2.astype(F32)))
    h = (hid @ w3.astype(F32)).reshape(L, HYENA_ORDER, 2, W_H)
    deltas = jnp.abs(jnp.linspace(math.log(1e-2) / 1.5, math.log(1e-2) / 0.3, W_H, dtype=F32))
    h = h * jnp.exp(-t[:, None] * deltas[None, :])[:, None, None, :]
    return h * lax.rsqrt(jnp.sum(h * h, axis=(0, 2), keepdims=True) + EPS)


def trunk_layer(x, mod, p, B, L, rows, states):
    sh1, sc1, g1, sh2, sc2, g2 = mod
    mC, mn, mm, gS = states
    hb = norm_mod(x, p['norm_mix'], sc1, sh1, L)
    z = matmul(hb, p['w_in_main'], tm=1024, tn=512)
    zs = matmul(hb, p['w_in_small'], tm=1024, tn=N_SMALL_PAD)
    filt = hyena_filters(L, p['hyena_w1'], p['hyena_b1'], p['hyena_w2'], p['hyena_b2'], p['hyena_freq'], p['hyena_w3'])
    hyena = hyena_fft_pallas if L >= FFT_MIN_L else hyena_pallas
    y_a = hyena(z, p['hyena_conv_w'], filt, p['hyena_bias'], p['dft'][L], B, L)
    y_b, m_state = mlstm_pallas(z, zs, p['mlstm_gate_b'], p['mlstm_norm'],
                                None if mC is None else (mC, mn, mm), B, L)
    y_c, g_state = gdn_pallas(z, zs, p['gdn_conv_w'], p['gdn_A_log'], p['gdn_dt_bias'], p['gdn_norm'], gS, B, L)
    mix = branch_mix(y_a, y_b, y_c, p['w_branch'], z)
    x = matmul_residual(mix, p['w_out'], x, g1, L, tm=1024, tn=512)
    hb = norm_mod(x, p['norm_ffn'], sc2, sh2, L)
    gate_pre = matmul(hb, p['ffn_w_gate'], tm=1024, tn=1024)
    up = matmul(hb, p['ffn_w_up'], tm=1024, tn=1024)
    act = ffn_act(gate_pre, up, p['ffn_conv_w'], p['ffn_conv_b'], B, L, rows)
    x = matmul_residual(act, p['ffn_w_down'], x, g2, L, tm=1024, tn=512, tk=D_FF_PAD // 2)
    return x, m_state + (g_state,)


def _ada_all(c, c_ctx, ada_w, ada_b):
    cc = jnp.concatenate([c, c_ctx[None, :], jnp.zeros((7, D_MODEL), F32)], axis=0)
    a = jax.nn.silu(cc)
    outs = []
    for l in range(DEPTH):
        outs.append(matmul(a, ada_w[l], tm=16, tn=1024) + ada_b[l])
    return outs


def kernel(x_prompt, x_sample, state_mlstm_C, state_mlstm_n, state_mlstm_m, state_gdn_S, c, c_ctx,
           ada_w, ada_b, norm_mix, norm_ffn, w_in, hyena_conv_w, hyena_w1, hyena_b1, hyena_w2, hyena_b2,
           hyena_freq, hyena_w3, hyena_bias, mlstm_gate_b, mlstm_norm, gdn_conv_w, gdn_A_log, gdn_dt_bias,
           gdn_norm, w_branch, w_out, ffn_w_gate, ffn_w_up, ffn_conv_w, ffn_conv_b, ffn_w_down, final_norm):
    bp, lp, _ = x_prompt.shape
    nb, ls, _ = x_sample.shape
    rows_lat = ls // GRID_W
    zero_states = (None, None, None, None)
    ada = _ada_all(c, c_ctx, ada_w, ada_b)
    ffpad = D_FF_PAD - D_FF
    dft = {L: fft_matrices(L) if L >= FFT_MIN_L else dft_matrices(L) for L in sorted({lp, ls})}
    xp, xs = x_prompt.reshape(bp * lp, D_MODEL), x_sample.reshape(nb * ls, D_MODEL)
    out_C, out_n, out_m, out_S = [], [], [], []
    for l in range(DEPTH):
        wl = w_in[l]
        w_main = jnp.concatenate([wl[:, :_OFF[5]], wl[:, _OFF[6]:_OFF[8]], wl[:, _OFF[10]:]], axis=1).astype(BF16)
        w_small = jnp.concatenate([wl[:, _OFF[5]:_OFF[6]], wl[:, _OFF[8]:_OFF[10]],
                                   jnp.zeros((D_MODEL, N_SMALL_PAD - 96), F32)], axis=1).astype(BF16)
        p = {'dft': dft, 'norm_mix': norm_mix[l], 'norm_ffn': norm_ffn[l], 'w_in_main': w_main, 'w_in_small': w_small,
             'hyena_conv_w': hyena_conv_w[l], 'hyena_w1': hyena_w1[l], 'hyena_b1': hyena_b1[l],
             'hyena_w2': hyena_w2[l], 'hyena_b2': hyena_b2[l], 'hyena_freq': hyena_freq[l],
             'hyena_w3': hyena_w3[l], 'hyena_bias': hyena_bias[l],
             'mlstm_gate_b': mlstm_gate_b[l], 'mlstm_norm': mlstm_norm[l],
             'gdn_conv_w': gdn_conv_w[l], 'gdn_A_log': gdn_A_log[l], 'gdn_dt_bias': gdn_dt_bias[l],
             'gdn_norm': gdn_norm[l], 'w_branch': w_branch[l].astype(BF16), 'w_out': w_out[l].astype(BF16),
             'ffn_w_gate': jnp.pad(ffn_w_gate[l], ((0, 0), (0, ffpad))).astype(BF16),
             'ffn_w_up': jnp.pad(ffn_w_up[l], ((0, 0), (0, ffpad))).astype(BF16),
             'ffn_conv_w': jnp.pad(ffn_conv_w[l], ((0, 0), (0, 0), (0, ffpad))),
             'ffn_conv_b': jnp.pad(ffn_conv_b[l], ((0, ffpad),)),
             'ffn_w_down': jnp.pad(ffn_w_down[l], ((0, ffpad), (0, 0))).astype(BF16)}
        m = ada[l]
        mod_ctx = [t[None, None, :] for t in jnp.split(m[nb], 6, axis=-1)]
        mod_lat = [t[:, None, :] for t in jnp.split(m[:nb], 6, axis=-1)]
        xp, (sC, sn, sm, sS) = trunk_layer(xp, mod_ctx, p, bp, lp, 1, zero_states)
        out_C.append(sC)
        out_n.append(sn)
        out_m.append(sm)
        out_S.append(sS)
        cached = (state_mlstm_C[:, l].astype(F32), state_mlstm_n[:, l].astype(F32),
                  state_mlstm_m[:, l].astype(F32), state_gdn_S[:, l].astype(F32))
        xs, _ = trunk_layer(xs, mod_lat, p, nb, ls, rows_lat, cached)
    zero = jnp.zeros((1, 1, D_MODEL), F32)
    y_prompt = norm_mod(xp, final_norm, zero, zero, lp, out_dtype=F32).reshape(bp, lp, D_MODEL)
    y_sample = norm_mod(xs, final_norm, zero, zero, ls, out_dtype=F32).reshape(nb, ls, D_MODEL)
    return (y_prompt, y_sample, jnp.stack(out_C, axis=1), jnp.stack(out_n, axis=1),
            jnp.stack(out_m, axis=1), jnp.stack(out_S, axis=1))
```
